```python
import math
import jax
import jax.numpy as jnp
from jax import lax
import numpy as np

D_MODEL = 1024
BATCH = 8
SEQ = 4096
DEPTH = 2

HEAD_DIM = 64
SB_HEADS = 8
SWA_HEADS = 8
SWA_KV_HEADS = 2
SWA_GROUP = SWA_HEADS // SWA_KV_HEADS
WINDOW = 128
BLOCK = 128
D_FF = 4 * D_MODEL
ROPE_THETA = 10000.0
EPS = 1e-6
NEG = -1e30

SB_W = SB_HEADS * HEAD_DIM
SWA_Q_W = SWA_HEADS * HEAD_DIM
SWA_KV_W = SWA_KV_HEADS * HEAD_DIM
MIX_W = SB_W + SWA_Q_W
IN_W = 3 * SB_W + SWA_Q_W + 2 * SWA_KV_W

kernel_name = "hybrid_stickbreak_swa_sink_block"


def rmsnorm(x, g):
    xf = x.astype(jnp.float32)
    y = xf * lax.rsqrt(jnp.mean(xf * xf, axis=-1, keepdims=True) + EPS)
    return (y * g.astype(jnp.float32)).astype(x.dtype)


def rope(x, positions):
    half = HEAD_DIM // 2
    inv_freq = 1.0 / (ROPE_THETA ** (jnp.arange(half, dtype=jnp.float32) * (2.0 / HEAD_DIM)))
    ang = positions.astype(jnp.float32)[:, None] * inv_freq[None, :]
    cos = jnp.cos(ang)[None, :, None, :]
    sin = jnp.sin(ang)[None, :, None, :]
    xf = x.astype(jnp.float32)
    x1, x2 = xf[..., :half], xf[..., half:]
    out = jnp.concatenate([x1 * cos - x2 * sin, x2 * cos + x1 * sin], axis=-1)
    return out.astype(x.dtype)


def stick_breaking_attention(q, k, v):
    B, S, H, D = q.shape
    scale = 1.0 / math.sqrt(D)
    qh = jnp.transpose(q, (0, 2, 1, 3))
    kh = jnp.transpose(k, (0, 2, 1, 3))
    vh = jnp.transpose(v, (0, 2, 1, 3)).astype(jnp.float32)
    outs = []
    for i in range(S // BLOCK):
        end = (i + 1) * BLOCK
        qb = qh[:, :, i * BLOCK:end]
        z = jnp.einsum('bhqd,bhkd->bhqk', qb, kh[:, :, :end]).astype(jnp.float32) * scale
        t_pos = i * BLOCK + jnp.arange(BLOCK)[:, None]
        s_pos = jnp.arange(end)[None, :]
        strict = s_pos < t_pos
        log_beta = jax.nn.log_sigmoid(z)
        log_1m = jnp.where(strict, -jax.nn.softplus(z), 0.0)
        suffix = lax.cumsum(log_1m, axis=3, reverse=True) - log_1m
        a = jnp.where(strict, jnp.exp(log_beta + suffix), 0.0)
        outs.append(jnp.einsum('bhqk,bhkd->bhqd', a, vh[:, :, :end]))
    o = jnp.concatenate(outs, axis=2)
    return jnp.transpose(o, (0, 2, 1, 3)).astype(v.dtype)


def sliding_window_sink_attention(q, k, v, sinks):
    B, S, Hq, D = q.shape
    nblk = S // BLOCK
    scale = 1.0 / math.sqrt(D)
    qb = q.reshape(B, nblk, BLOCK, SWA_KV_HEADS, SWA_GROUP, D)
    pad = ((0, 0), (BLOCK, 0), (0, 0), (0, 0))
    kp = jnp.pad(k, pad).reshape(B, nblk + 1, BLOCK, SWA_KV_HEADS, D)
    vp = jnp.pad(v, pad).reshape(B, nblk + 1, BLOCK, SWA_KV_HEADS, D)
    kb = jnp.concatenate([kp[:, :-1], kp[:, 1:]], axis=2)
    vb = jnp.concatenate([vp[:, :-1], vp[:, 1:]], axis=2)
    s = jnp.einsum('bnqkgd,bnskd->bnkgqs', qb, kb).astype(jnp.float32) * scale
    q_rel = BLOCK + jnp.arange(BLOCK)[:, None]
    k_rel = jnp.arange(2 * BLOCK)[None, :]
    diff = q_rel - k_rel
    k_abs = (jnp.arange(nblk)[:, None, None] - 1) * BLOCK + k_rel[None]
    mask = (diff >= 0)[None] & (diff < WINDOW)[None] & (k_abs >= 0)
    mask = mask[None, :, None, None]
    s = jnp.where(mask, s, NEG)
    sink = sinks.astype(jnp.float32).reshape(1, 1, SWA_KV_HEADS, SWA_GROUP, 1, 1)
    m = jnp.maximum(jnp.max(s, axis=-1, keepdims=True), sink)
    p = jnp.where(mask, jnp.exp(s - m), 0.0)
    probs = p / (jnp.sum(p, axis=-1, keepdims=True) + jnp.exp(sink - m))
    o = jnp.einsum('bnkgqs,bnskd->bnqkgd', probs, vb.astype(jnp.float32))
    return o.reshape(B, S, Hq, D).astype(v.dtype)


def setup_inputs(seed: int = 0) -> dict:
    key = jax.random.key(seed)
    ks = jax.random.split(key, 12)
    f32 = jnp.float32
    x = jax.random.normal(ks[0], (BATCH, SEQ, D_MODEL), f32)
    positions = jnp.arange(SEQ, dtype=jnp.int32)
    attn_norm = 1.0 + 0.02 * jax.random.normal(ks[1], (DEPTH, D_MODEL), f32)
    w_in = jax.random.normal(ks[2], (DEPTH, D_MODEL, IN_W), f32) * D_MODEL ** -0.5
    sb_norm = 1.0 + 0.02 * jax.random.normal(ks[3], (DEPTH, SB_W), f32)
    swa_norm = 1.0 + 0.02 * jax.random.normal(ks[4], (DEPTH, SWA_Q_W), f32)
    sinks = 0.5 * jax.random.normal(ks[5], (DEPTH, SWA_HEADS), f32)
    w_out = jax.random.normal(ks[6], (DEPTH, MIX_W, D_MODEL), f32) * MIX_W ** -0.5
    mlp_norm = 1.0 + 0.02 * jax.random.normal(ks[7], (DEPTH, D_MODEL), f32)
    w_up = jax.random.normal(ks[8], (DEPTH, D_MODEL, D_FF), f32) * D_MODEL ** -0.5
    w_down = jax.random.normal(ks[9], (DEPTH, D_FF, D_MODEL), f32) * D_FF ** -0.5
    final_norm = 1.0 + 0.02 * jax.random.normal(ks[10], (D_MODEL,), f32)
    return {"x": x, "positions": positions, "attn_norm": attn_norm, "w_in": w_in,
            "sb_norm": sb_norm, "swa_norm": swa_norm, "sinks": sinks, "w_out": w_out,
            "mlp_norm": mlp_norm, "w_up": w_up, "w_down": w_down, "final_norm": final_norm}


def reference(x, positions, attn_norm, w_in, sb_norm, swa_norm, sinks, w_out,
              mlp_norm, w_up, w_down, final_norm):
    B, S, _ = x.shape
    splits = np.cumsum([SB_W, SB_W, SB_W, SWA_Q_W, SWA_KV_W]).tolist()
    for l in range(DEPTH):
        h = rmsnorm(x, attn_norm[l])
        proj = jnp.einsum('bsd,de->bse', h, w_in[l])
        sb_q, sb_k, sb_v, sw_q, sw_k, sw_v = jnp.split(proj, splits, axis=-1)
        sb_o = stick_breaking_attention(sb_q.reshape(B, S, SB_HEADS, HEAD_DIM),
                                        sb_k.reshape(B, S, SB_HEADS, HEAD_DIM),
                                        sb_v.reshape(B, S, SB_HEADS, HEAD_DIM))
        sw_q = rope(sw_q.reshape(B, S, SWA_HEADS, HEAD_DIM), positions)
        sw_k = rope(sw_k.reshape(B, S, SWA_KV_HEADS, HEAD_DIM), positions)
        sw_o = sliding_window_sink_attention(sw_q, sw_k,
                                             sw_v.reshape(B, S, SWA_KV_HEADS, HEAD_DIM),
                                             sinks[l])
        mix = jnp.concatenate([rmsnorm(sb_o.reshape(B, S, SB_W), sb_norm[l]),
                               rmsnorm(sw_o.reshape(B, S, SWA_Q_W), swa_norm[l])], axis=-1)
        x = x + jnp.einsum('bse,ed->bsd', mix, w_out[l]).astype(x.dtype)
        h = rmsnorm(x, mlp_norm[l])
        u = jnp.square(jax.nn.relu(jnp.einsum('bsd,df->bsf', h, w_up[l])))
        x = x + jnp.einsum('bsf,fd->bsd', u, w_down[l]).astype(x.dtype)
    return rmsnorm(x, final_norm)
```

```python
import functools
import math

import jax
import jax.numpy as jnp
from jax import lax
from jax.experimental import pallas as pl
from jax.experimental.pallas import tpu as pltpu

D_MODEL = 1024
HEAD_DIM = 64
SB_HEADS = 8
SWA_HEADS = 8
SWA_KV_HEADS = 2
SWA_GROUP = SWA_HEADS // SWA_KV_HEADS
BLOCK = 128
D_FF = 4 * D_MODEL
ROPE_THETA = 10000.0
EPS = 1e-6
NEG = -1e30

SB_W = SB_HEADS * HEAD_DIM
SWA_Q_W = SWA_HEADS * HEAD_DIM
SWA_KV_W = SWA_KV_HEADS * HEAD_DIM
MIX_W = SB_W + SWA_Q_W

LANES = 128
VMEM_LIMIT_BYTES = 56 * 1024 * 1024

PROJ_TILE = 512
MLP_TILE = 512
FF_CHUNK = 1024

BF16 = jnp.bfloat16
F32 = jnp.float32


def _rms(x, g):
    return x * lax.rsqrt(jnp.mean(x * x, axis=-1, keepdims=True) + EPS) * g


def _rope_table_kernel(pos_ref, freq_ref, cos_ref, sin_ref):
    ang = pos_ref[...].astype(F32) * freq_ref[...]
    lane = lax.broadcasted_iota(jnp.int32, ang.shape, 1)
    first_half = (lane % HEAD_DIM) < (HEAD_DIM // 2)
    cos_ref[...] = jnp.cos(ang)
    s = jnp.sin(ang)
    sin_ref[...] = jnp.where(first_half, -s, s)


def _rope_tables(positions):
    s = positions.shape[0]
    half = HEAD_DIM // 2
    inv_freq = 1.0 / (ROPE_THETA ** (jnp.arange(half, dtype=F32) * (2.0 / HEAD_DIM)))
    freq = jnp.tile(inv_freq, LANES // half).reshape(1, LANES)
    return pl.pallas_call(
        _rope_table_kernel,
        out_shape=(jax.ShapeDtypeStruct((s, LANES), F32),
                   jax.ShapeDtypeStruct((s, LANES), F32)),
        name="rope_tables",
    )(positions.reshape(s, 1), freq)


def _rope(x, cos, sin_signed):
    lane = lax.broadcasted_iota(jnp.int32, (x.shape[0], LANES), 1)
    first_half = (lane % HEAD_DIM) < (HEAD_DIM // 2)
    half = HEAD_DIM // 2
    outs = []
    for c in range(x.shape[1] // LANES):
        xc = x[:, c * LANES:(c + 1) * LANES]
        swapped = jnp.where(first_half,
                            pltpu.roll(xc, LANES - half, axis=1),
                            pltpu.roll(xc, half, axis=1))
        outs.append(xc * cos + swapped * sin_signed)
    return jnp.concatenate(outs, axis=1) if len(outs) > 1 else outs[0]


def _in_proj_kernel(x_ref, g_ref, cos_ref, sin_ref, wq_ref, wkt_ref, wv_ref,
                    wsq_ref, wsk_ref, wsv_ref,
                    sbq_ref, sbkt_ref, sbv_ref, swq_ref, swk_ref, swv_ref):
    h = _rms(x_ref[0], g_ref[...]).astype(BF16)
    cos = cos_ref[...]
    sin = sin_ref[...]
    dot = functools.partial(jnp.dot, preferred_element_type=F32)
    sbq_ref[0] = dot(h, wq_ref[...]).astype(BF16)
    sbkt_ref[0] = lax.dot_general(wkt_ref[...], h, (((1,), (1,)), ((), ())),
                                  preferred_element_type=F32).astype(BF16)
    sbv_ref[0] = dot(h, wv_ref[...]).astype(BF16)
    swq_ref[0] = _rope(dot(h, wsq_ref[...]), cos, sin).astype(BF16)
    swk_ref[0] = _rope(dot(h, wsk_ref[...]), cos, sin).astype(BF16)
    swv_ref[0] = dot(h, wsv_ref[...]).astype(BF16)


def _in_proj(x, g, cos, sin, wq, wkt, wv, wsq, wsk, wsv):
    b, s, d = x.shape
    t = PROJ_TILE
    const = lambda bi, ti: (0, 0)
    tok = lambda w: pl.BlockSpec((1, t, w), lambda bi, ti: (bi, ti, 0))
    out_shape = (
        jax.ShapeDtypeStruct((b, s, SB_W), BF16),
        jax.ShapeDtypeStruct((b, SB_W, s), BF16),
        jax.ShapeDtypeStruct((b, s, SB_W), BF16),
        jax.ShapeDtypeStruct((b, s, SWA_Q_W), BF16),
        jax.ShapeDtypeStruct((b, s, SWA_KV_W), BF16),
        jax.ShapeDtypeStruct((b, s, SWA_KV_W), BF16),
    )
    return pl.pallas_call(
        _in_proj_kernel,
        grid=(b, s // t),
        in_specs=[
            tok(d),
            pl.BlockSpec((1, d), const),
            pl.BlockSpec((t, LANES), lambda bi, ti: (ti, 0)),
            pl.BlockSpec((t, LANES), lambda bi, ti: (ti, 0)),
            pl.BlockSpec(wq.shape, const),
            pl.BlockSpec(wkt.shape, const),
            pl.BlockSpec(wv.shape, const),
            pl.BlockSpec(wsq.shape, const),
            pl.BlockSpec(wsk.shape, const),
            pl.BlockSpec(wsv.shape, const),
        ],
        out_specs=(
            tok(SB_W),
            pl.BlockSpec((1, SB_W, t), lambda bi, ti: (bi, 0, ti)),
            tok(SB_W), tok(SWA_Q_W), tok(SWA_KV_W), tok(SWA_KV_W),
        ),
        out_shape=out_shape,
        compiler_params=pltpu.CompilerParams(
            dimension_semantics=("arbitrary", "arbitrary"),
            vmem_limit_bytes=VMEM_LIMIT_BYTES),
        name="in_proj",
    )(x, g, cos, sin, wq, wkt, wv, wsq, wsk, wsv)


def _stick_breaking_pair(i, q_pair, kt_ref, v_ref, pair, tri):
    lane = lax.broadcasted_iota(jnp.int32, (BLOCK, LANES), 1)
    row = lax.broadcasted_iota(jnp.int32, (BLOCK, BLOCK), 0)
    col = lax.broadcasted_iota(jnp.int32, (BLOCK, BLOCK), 1)
    zero = jnp.zeros_like(q_pair)
    qs = (jnp.where(lane < HEAD_DIM, q_pair, zero), jnp.where(lane < HEAD_DIM, zero, q_pair))
    c0 = pair * LANES

    def body(step, carry):
        j = i - step
        k0 = pl.multiple_of(j * BLOCK, BLOCK)
        kt = kt_ref[0, c0:c0 + LANES, pl.ds(k0, BLOCK)]
        v = v_ref[0, pl.ds(k0, BLOCK), c0:c0 + LANES]
        strict = (col + j * BLOCK) < (row + i * BLOCK)
        new = []
        for h in range(2):
            acc, run = carry[2 * h], carry[2 * h + 1]
            z = jnp.dot(qs[h], kt, preferred_element_type=F32)
            sp = jnp.maximum(z, 0.0) + jnp.log(1.0 + jnp.exp(-jnp.abs(z)))
            sp = jnp.where(strict, sp, 0.0)
            hi = sp.astype(BF16)
            lo = (sp - hi.astype(F32)).astype(BF16)
            incl = jnp.dot(jnp.concatenate([hi, lo], axis=1), tri,
                           preferred_element_type=F32)
            a = jnp.where(strict, jnp.exp(z + incl + run), 0.0)
            acc = acc + jnp.dot(a.astype(BF16), v, preferred_element_type=F32)
            run = run + incl[:, 0:1]
            new += [acc, run]
        return tuple(new)

    init = (jnp.zeros((BLOCK, LANES), F32), jnp.zeros((BLOCK, 1), F32)) * 2
    acc0, _, acc1, _ = lax.fori_loop(0, i + 1, body, init)
    return jnp.where(lane < HEAD_DIM, acc0, acc1)


def _swa_group(i, q_grp, sink_col, k_ref, v_ref, g):
    rows = SWA_GROUP * BLOCK
    cur0 = pl.multiple_of(i * BLOCK, BLOCK)
    prev0 = pl.multiple_of(jnp.maximum(i - 1, 0) * BLOCK, BLOCK)
    lo, hi = g * HEAD_DIM, (g + 1) * HEAD_DIM
    k_cur = k_ref[0, pl.ds(cur0, BLOCK), lo:hi]
    k_prev = k_ref[0, pl.ds(prev0, BLOCK), lo:hi]
    v_cur = v_ref[0, pl.ds(cur0, BLOCK), lo:hi]
    v_prev = v_ref[0, pl.ds(prev0, BLOCK), lo:hi]
    nt = (((1,), (1,)), ((), ()))
    s_cur = lax.dot_general(q_grp, k_cur, nt, preferred_element_type=F32)
    s_prev = lax.dot_general(q_grp, k_prev, nt, preferred_element_type=F32)
    qpos = lax.broadcasted_iota(jnp.int32, (rows, BLOCK), 0) % BLOCK
    kpos = lax.broadcasted_iota(jnp.int32, (rows, BLOCK), 1)
    m_cur = kpos <= qpos
    m_prev = (kpos > qpos) & (i > 0)
    s_cur = jnp.where(m_cur, s_cur, NEG)
    s_prev = jnp.where(m_prev, s_prev, NEG)
    m = jnp.maximum(jnp.maximum(jnp.max(s_cur, axis=1, keepdims=True),
                                jnp.max(s_prev, axis=1, keepdims=True)), sink_col)
    p_cur = jnp.where(m_cur, jnp.exp(s_cur - m), 0.0)
    p_prev = jnp.where(m_prev, jnp.exp(s_prev - m), 0.0)
    denom = (jnp.sum(p_cur, axis=1, keepdims=True) + jnp.sum(p_prev, axis=1, keepdims=True)
             + jnp.exp(sink_col - m))
    o = (jnp.dot(p_cur.astype(BF16), v_cur, preferred_element_type=F32)
         + jnp.dot(p_prev.astype(BF16), v_prev, preferred_element_type=F32))
    return o / denom


def _mixer_kernel(sinks_ref, x_ref, sbq_ref, sbkt_ref, sbv_ref, swq_ref, swk_ref, swv_ref,
                  tri_ref, gsb_ref, gsw_ref, wout_ref, o_ref):
    i = pl.program_id(1)
    tri = tri_ref[...]
    sb = [_stick_breaking_pair(i, sbq_ref[0, :, p * LANES:(p + 1) * LANES],
                               sbkt_ref, sbv_ref, p, tri)
          for p in range(SB_HEADS // 2)]
    sb_o = jnp.concatenate(sb, axis=1)
    sw = []
    for g in range(SWA_KV_HEADS):
        heads = range(g * SWA_GROUP, (g + 1) * SWA_GROUP)
        q_grp = jnp.concatenate(
            [swq_ref[0, :, h * HEAD_DIM:(h + 1) * HEAD_DIM] for h in heads], axis=0)
        sink_col = jnp.concatenate(
            [jnp.full((BLOCK, 1), sinks_ref[h], F32) for h in heads], axis=0)
        o = _swa_group(i, q_grp, sink_col, swk_ref, swv_ref, g)
        sw += [o[n * BLOCK:(n + 1) * BLOCK] for n in range(SWA_GROUP)]
    sw_o = jnp.concatenate(sw, axis=1)
    mix = jnp.concatenate([_rms(sb_o, gsb_ref[...]), _rms(sw_o, gsw_ref[...])], axis=1)
    o_ref[0] = x_ref[0] + jnp.dot(mix.astype(BF16), wout_ref[...],
                                  preferred_element_type=F32)


def _mixer(x, sinks, sbq, sbkt, sbv, swq, swk, swv, tri, gsb, gsw, wout):
    b, s, d = x.shape
    const = lambda bi, qi: (0, 0)
    qblk = lambda w: pl.BlockSpec((1, BLOCK, w), lambda bi, qi: (bi, qi, 0))
    per_batch = lambda r, c: pl.BlockSpec((1, r, c), lambda bi, qi: (bi, 0, 0))
    return pl.pallas_call(
        _mixer_kernel,
        grid=(b, s // BLOCK),
        in_specs=[
            pl.BlockSpec(memory_space=pltpu.SMEM),
            qblk(d),
            qblk(SB_W),
            per_batch(SB_W, s),
            per_batch(s, SB_W),
            qblk(SWA_Q_W),
            per_batch(s, SWA_KV_W),
            per_batch(s, SWA_KV_W),
            pl.BlockSpec(tri.shape, const),
            pl.BlockSpec((1, SB_W), const),
            pl.BlockSpec((1, SWA_Q_W), const),
            pl.BlockSpec(wout.shape, const),
        ],
        out_specs=qblk(d),
        out_shape=jax.ShapeDtypeStruct((b, s, d), F32),
        compiler_params=pltpu.CompilerParams(
            dimension_semantics=("arbitrary", "arbitrary"),
            vmem_limit_bytes=VMEM_LIMIT_BYTES),
        name="mixer",
    )(sinks, x, sbq, sbkt, sbv, swq, swk, swv, tri, gsb, gsw, wout)


def _mlp_kernel(x_ref, g_ref, wup_ref, wdown_ref, gfin_ref, o_ref, *, final_norm):
    x = x_ref[...]
    h = _rms(x, g_ref[...]).astype(BF16)
    acc = x
    for c in range(D_FF // FF_CHUNK):
        u = jnp.dot(h, wup_ref[:, c * FF_CHUNK:(c + 1) * FF_CHUNK],
                    preferred_element_type=F32)
        u = jnp.square(jnp.maximum(u, 0.0)).astype(BF16)
        acc = acc + jnp.dot(u, wdown_ref[c * FF_CHUNK:(c + 1) * FF_CHUNK, :],
                            preferred_element_type=F32)
    if final_norm:
        acc = _rms(acc, gfin_ref[...])
    o_ref[...] = acc


def _mlp(x2d, g, wup, wdown, gfin, final_norm):
    n, d = x2d.shape
    t = MLP_TILE
    const = lambda ti: (0, 0)
    return pl.pallas_call(
        functools.partial(_mlp_kernel, final_norm=final_norm),
        grid=(n // t,),
        in_specs=[
            pl.BlockSpec((t, d), lambda ti: (ti, 0)),
            pl.BlockSpec((1, d), const),
            pl.BlockSpec(wup.shape, const),
            pl.BlockSpec(wdown.shape, const),
            pl.BlockSpec((1, d), const),
        ],
        out_specs=pl.BlockSpec((t, d), lambda ti: (ti, 0)),
        out_shape=jax.ShapeDtypeStruct((n, d), F32),
        compiler_params=pltpu.CompilerParams(
            dimension_semantics=("arbitrary",),
            vmem_limit_bytes=VMEM_LIMIT_BYTES),
        name="mlp",
    )(x2d, g, wup, wdown, gfin)


def kernel(x, positions, attn_norm, w_in, sb_norm, swa_norm, sinks, w_out, mlp_norm, w_up,
           w_down, final_norm):
    b, s, d = x.shape
    depth = w_in.shape[0]
    scale = 1.0 / math.sqrt(HEAD_DIM)
    cos, sin = _rope_tables(positions)

    r = lax.broadcasted_iota(jnp.int32, (BLOCK, BLOCK), 0)
    c = lax.broadcasted_iota(jnp.int32, (BLOCK, BLOCK), 1)
    tri = jnp.where(r >= c, -1.0, 0.0).astype(BF16)
    tri = jnp.concatenate([tri, tri], axis=0)

    o = np_cumsum_splits()
    for l in range(depth):
        w = w_in[l]
        wq = (w[:, o[0]:o[1]] * scale).astype(BF16)
        wkt = w[:, o[1]:o[2]].T.astype(BF16)
        wv = w[:, o[2]:o[3]].astype(BF16)
        wsq = (w[:, o[3]:o[4]] * scale).astype(BF16)
        wsk = w[:, o[4]:o[5]].astype(BF16)
        wsv = w[:, o[5]:o[6]].astype(BF16)
        sbq, sbkt, sbv, swq, swk, swv = _in_proj(
            x, attn_norm[l].reshape(1, d), cos, sin, wq, wkt, wv, wsq, wsk, wsv)
        x = _mixer(x, sinks[l], sbq, sbkt, sbv, swq, swk, swv, tri,
                   sb_norm[l].reshape(1, SB_W), swa_norm[l].reshape(1, SWA_Q_W),
                   w_out[l].astype(BF16))
        x = _mlp(x.reshape(b * s, d), mlp_norm[l].reshape(1, d), w_up[l].astype(BF16),
                 w_down[l].astype(BF16), final_norm.reshape(1, d),
                 final_norm=(l == depth - 1)).reshape(b, s, d)
    return x


def np_cumsum_splits():
    widths = [SB_W, SB_W, SB_W, SWA_Q_W, SWA_KV_W, SWA_KV_W]
    offs = [0]
    for w in widths:
        offs.append(offs[-1] + w)
    return offs
```

```python
import functools
import math

import jax
import jax.numpy as jnp
from jax import lax
from jax.experimental import pallas as pl
from jax.experimental.pallas import tpu as pltpu

D_MODEL = 1024
HEAD_DIM = 64
SB_HEADS = 8
SWA_HEADS = 8
SWA_KV_HEADS = 2
SWA_GROUP = SWA_HEADS // SWA_KV_HEADS
BLOCK = 128
D_FF = 4 * D_MODEL
ROPE_THETA = 10000.0
EPS = 1e-6
NEG = -1e30
EXP_ZERO_BELOW = -105.0

SB_W = SB_HEADS * HEAD_DIM
SWA_Q_W = SWA_HEADS * HEAD_DIM
SWA_KV_W = SWA_KV_HEADS * HEAD_DIM
MIX_W = SB_W + SWA_Q_W

LANES = 128
VMEM_LIMIT_BYTES = 56 * 1024 * 1024

PROJ_TILE = 512
MLP_TILE = 512
FF_CHUNK = 1024

BF16 = jnp.bfloat16
F32 = jnp.float32


def _rms(x, g):
    return x * lax.rsqrt(jnp.mean(x * x, axis=-1, keepdims=True) + EPS) * g


def _rope_table_kernel(pos_ref, freq_ref, cos_ref, sin_ref):
    ang = pos_ref[...].astype(F32) * freq_ref[...]
    lane = lax.broadcasted_iota(jnp.int32, ang.shape, 1)
    first_half = (lane % HEAD_DIM) < (HEAD_DIM // 2)
    cos_ref[...] = jnp.cos(ang)
    s = jnp.sin(ang)
    sin_ref[...] = jnp.where(first_half, -s, s)


def _rope_tables(positions):
    s = positions.shape[0]
    half = HEAD_DIM // 2
    inv_freq = 1.0 / (ROPE_THETA ** (jnp.arange(half, dtype=F32) * (2.0 / HEAD_DIM)))
    freq = jnp.tile(inv_freq, LANES // half).reshape(1, LANES)
    return pl.pallas_call(
        _rope_table_kernel,
        out_shape=(jax.ShapeDtypeStruct((s, LANES), F32),
                   jax.ShapeDtypeStruct((s, LANES), F32)),
        name="rope_tables",
    )(positions.reshape(s, 1), freq)


def _rope(x, cos, sin_signed):
    lane = lax.broadcasted_iota(jnp.int32, (x.shape[0], LANES), 1)
    first_half = (lane % HEAD_DIM) < (HEAD_DIM // 2)
    half = HEAD_DIM // 2
    outs = []
    for c in range(x.shape[1] // LANES):
        xc = x[:, c * LANES:(c + 1) * LANES]
        swapped = jnp.where(first_half,
                            pltpu.roll(xc, LANES - half, axis=1),
                            pltpu.roll(xc, half, axis=1))
        outs.append(xc * cos + swapped * sin_signed)
    return jnp.concatenate(outs, axis=1) if len(outs) > 1 else outs[0]


def _in_proj_kernel(x_ref, g_ref, cos_ref, sin_ref, wq_ref, wkt_ref, wv_ref,
                    wsq_ref, wsk_ref, wsv_ref,
                    sbq_ref, sbkt_ref, sbv_ref, swq_ref, swk_ref, swv_ref):
    h = _rms(x_ref[0], g_ref[...]).astype(BF16)
    cos = cos_ref[...]
    sin = sin_ref[...]
    dot = functools.partial(jnp.dot, preferred_element_type=F32)
    sbq_ref[0] = dot(h, wq_ref[...]).astype(BF16)
    sbkt_ref[0] = lax.dot_general(wkt_ref[...], h, (((1,), (1,)), ((), ())),
                                  preferred_element_type=F32).astype(BF16)
    sbv_ref[0] = dot(h, wv_ref[...]).astype(BF16)
    swq_ref[0] = _rope(dot(h, wsq_ref[...]), cos, sin).astype(BF16)
    swk_ref[0] = _rope(dot(h, wsk_ref[...]), cos, sin).astype(BF16)
    swv_ref[0] = dot(h, wsv_ref[...]).astype(BF16)


def _in_proj(x, g, cos, sin, wq, wkt, wv, wsq, wsk, wsv):
    b, s, d = x.shape
    t = PROJ_TILE
    const = lambda bi, ti: (0, 0)
    tok = lambda w: pl.BlockSpec((1, t, w), lambda bi, ti: (bi, ti, 0))
    out_shape = (
        jax.ShapeDtypeStruct((b, s, SB_W), BF16),
        jax.ShapeDtypeStruct((b, SB_W, s), BF16),
        jax.ShapeDtypeStruct((b, s, SB_W), BF16),
        jax.ShapeDtypeStruct((b, s, SWA_Q_W), BF16),
        jax.ShapeDtypeStruct((b, s, SWA_KV_W), BF16),
        jax.ShapeDtypeStruct((b, s, SWA_KV_W), BF16),
    )
    return pl.pallas_call(
        _in_proj_kernel,
        grid=(b, s // t),
        in_specs=[
            tok(d),
            pl.BlockSpec((1, d), const),
            pl.BlockSpec((t, LANES), lambda bi, ti: (ti, 0)),
            pl.BlockSpec((t, LANES), lambda bi, ti: (ti, 0)),
            pl.BlockSpec(wq.shape, const),
            pl.BlockSpec(wkt.shape, const),
            pl.BlockSpec(wv.shape, const),
            pl.BlockSpec(wsq.shape, const),
            pl.BlockSpec(wsk.shape, const),
            pl.BlockSpec(wsv.shape, const),
        ],
        out_specs=(
            tok(SB_W),
            pl.BlockSpec((1, SB_W, t), lambda bi, ti: (bi, 0, ti)),
            tok(SB_W), tok(SWA_Q_W), tok(SWA_KV_W), tok(SWA_KV_W),
        ),
        out_shape=out_shape,
        compiler_params=pltpu.CompilerParams(
            dimension_semantics=("arbitrary", "arbitrary"),
            vmem_limit_bytes=VMEM_LIMIT_BYTES),
        name="in_proj",
    )(x, g, cos, sin, wq, wkt, wv, wsq, wsk, wsv)


def _stick_breaking(i, sbq_ref, kt_ref, v_ref, tri, acc_ref, run_ref):
    pairs = SB_HEADS // 2
    diff = (lax.broadcasted_iota(jnp.int32, (BLOCK, BLOCK), 1)
            - lax.broadcasted_iota(jnp.int32, (BLOCK, BLOCK), 0))
    sub_lo = lax.broadcasted_iota(jnp.int32, (LANES, BLOCK), 0) < HEAD_DIM
    lane_lo = lax.broadcasted_iota(jnp.int32, (BLOCK, LANES), 1) < HEAD_DIM
    q = [sbq_ref[0, :, p * LANES:(p + 1) * LANES] for p in range(pairs)]
    acc_ref[...] = jnp.zeros_like(acc_ref)
    run_ref[...] = jnp.zeros_like(run_ref)

    def tile(j):
        run = run_ref[...]
        k0 = pl.multiple_of(j * BLOCK, BLOCK)
        strict = (diff < (i - j) * BLOCK)[None]
        zs = []
        for p in range(pairs):
            kt = kt_ref[0, p * LANES:(p + 1) * LANES, pl.ds(k0, BLOCK)]
            zero = jnp.zeros_like(kt)
            kbd = jnp.concatenate([jnp.where(sub_lo, kt, zero), jnp.where(sub_lo, zero, kt)],
                                  axis=1)
            zp = jnp.dot(q[p], kbd, preferred_element_type=F32)
            zs += [zp[:, :BLOCK], zp[:, BLOCK:]]
        z = jnp.stack(zs)
        sp = jnp.maximum(z, 0.0) + jnp.log(1.0 + jnp.exp(-jnp.abs(z)))
        sp = jnp.where(strict, sp, 0.0).reshape(SB_HEADS * BLOCK, BLOCK)
        hi = sp.astype(BF16)
        lo = (sp - hi.astype(F32)).astype(BF16)
        cs = jnp.dot(jnp.concatenate([hi, lo], axis=1), tri, preferred_element_type=F32)
        incl = cs[:, :BLOCK].reshape(SB_HEADS, BLOCK, BLOCK)
        total = cs[:, BLOCK:].reshape(SB_HEADS, BLOCK, BLOCK)
        a = jnp.where(strict, jnp.exp(z + incl + run), 0.0).astype(BF16)
        for p in range(pairs):
            v = v_ref[0, pl.ds(k0, BLOCK), p * LANES:(p + 1) * LANES]
            zero = jnp.zeros_like(v)
            vbd = jnp.concatenate([jnp.where(lane_lo, v, zero), jnp.where(lane_lo, zero, v)],
                                  axis=0)
            a_pair = jnp.concatenate([a[2 * p], a[2 * p + 1]], axis=1)
            acc_ref[:, p * LANES:(p + 1) * LANES] += jnp.dot(
                a_pair, vbd, preferred_element_type=F32)
        run = run + total
        run_ref[...] = run
        return jnp.max(run)

    def cond(carry):
        return carry[1]

    def body(carry):
        j, _ = carry
        return j - 1, (j > 0) & (tile(j) > EXP_ZERO_BELOW)

    lax.while_loop(cond, body, (i, jnp.bool_(True)))


def _swa_group(i, q_grp, sink_col, k_ref, v_ref, g):
    rows = SWA_GROUP * BLOCK
    cur0 = pl.multiple_of(i * BLOCK, BLOCK)
    prev0 = pl.multiple_of(jnp.maximum(i - 1, 0) * BLOCK, BLOCK)
    lo, hi = g * HEAD_DIM, (g + 1) * HEAD_DIM
    k_cur = k_ref[0, pl.ds(cur0, BLOCK), lo:hi]
    k_prev = k_ref[0, pl.ds(prev0, BLOCK), lo:hi]
    own = (lax.broadcasted_iota(jnp.int32, (BLOCK, SWA_KV_W), 1) // HEAD_DIM) == g
    v_cur = v_ref[0, pl.ds(cur0, BLOCK), :]
    v_prev = v_ref[0, pl.ds(prev0, BLOCK), :]
    v_cur = jnp.where(own, v_cur, jnp.ones_like(v_cur))
    v_prev = jnp.where(own, v_prev, jnp.ones_like(v_prev))
    ones_lane = (1 - g) * HEAD_DIM
    nt = (((1,), (1,)), ((), ()))
    s_cur = lax.dot_general(q_grp, k_cur, nt, preferred_element_type=F32)
    s_prev = lax.dot_general(q_grp, k_prev, nt, preferred_element_type=F32)
    qpos = lax.broadcasted_iota(jnp.int32, (rows, BLOCK), 0) % BLOCK
    kpos = lax.broadcasted_iota(jnp.int32, (rows, BLOCK), 1)
    m_cur = kpos <= qpos
    m_prev = (kpos > qpos) & (i > 0)
    s_cur = jnp.where(m_cur, s_cur, NEG)
    s_prev = jnp.where(m_prev, s_prev, NEG)
    m = jnp.maximum(jnp.max(jnp.maximum(s_cur, s_prev), axis=1, keepdims=True), sink_col)
    p_cur = jnp.where(m_cur, jnp.exp(s_cur - m), 0.0)
    p_prev = jnp.where(m_prev, jnp.exp(s_prev - m), 0.0)
    o = (jnp.dot(p_cur.astype(BF16), v_cur, preferred_element_type=F32)
         + jnp.dot(p_prev.astype(BF16), v_prev, preferred_element_type=F32))
    denom = o[:, ones_lane:ones_lane + 1] + jnp.exp(sink_col - m)
    return o[:, lo:hi] / denom


def _mixer_kernel(sinks_ref, x_ref, sbq_ref, sbkt_ref, sbv_ref, swq_ref, swk_ref, swv_ref,
                  tri_ref, gsb_ref, gsw_ref, wout_ref, o_ref, acc_ref, run_ref):
    i = pl.program_id(1)
    _stick_breaking(i, sbq_ref, sbkt_ref, sbv_ref, tri_ref[...], acc_ref, run_ref)
    sb_o = acc_ref[...]
    sw = []
    for g in range(SWA_KV_HEADS):
        heads = range(g * SWA_GROUP, (g + 1) * SWA_GROUP)
        q_grp = jnp.concatenate(
            [swq_ref[0, :, h * HEAD_DIM:(h + 1) * HEAD_DIM] for h in heads], axis=0)
        sink_col = jnp.concatenate(
            [jnp.full((BLOCK, 1), sinks_ref[h], F32) for h in heads], axis=0)
        o = _swa_group(i, q_grp, sink_col, swk_ref, swv_ref, g)
        sw += [o[n * BLOCK:(n + 1) * BLOCK] for n in range(SWA_GROUP)]
    sw_o = jnp.concatenate(sw, axis=1)
    mix = jnp.concatenate([_rms(sb_o, gsb_ref[...]), _rms(sw_o, gsw_ref[...])], axis=1)
    o_ref[0] = x_ref[0] + jnp.dot(mix.astype(BF16), wout_ref[...],
                                  preferred_element_type=F32)


def _mixer(x, sinks, sbq, sbkt, sbv, swq, swk, swv, tri, gsb, gsw, wout):
    b, s, d = x.shape
    const = lambda bi, qi: (0, 0)
    qblk = lambda w: pl.BlockSpec((1, BLOCK, w), lambda bi, qi: (bi, qi, 0))
    per_batch = lambda r, c: pl.BlockSpec((1, r, c), lambda bi, qi: (bi, 0, 0))
    return pl.pallas_call(
        _mixer_kernel,
        grid=(b, s // BLOCK),
        in_specs=[
            pl.BlockSpec(memory_space=pltpu.SMEM),
            qblk(d),
            qblk(SB_W),
            per_batch(SB_W, s),
            per_batch(s, SB_W),
            qblk(SWA_Q_W),
            per_batch(s, SWA_KV_W),
            per_batch(s, SWA_KV_W),
            pl.BlockSpec(tri.shape, const),
            pl.BlockSpec((1, SB_W), const),
            pl.BlockSpec((1, SWA_Q_W), const),
            pl.BlockSpec(wout.shape, const),
        ],
        out_specs=qblk(d),
        out_shape=jax.ShapeDtypeStruct((b, s, d), F32),
        scratch_shapes=[pltpu.VMEM((BLOCK, SB_W), F32),
                        pltpu.VMEM((SB_HEADS, BLOCK, BLOCK), F32)],
        compiler_params=pltpu.CompilerParams(
            dimension_semantics=("arbitrary", "arbitrary"),
            vmem_limit_bytes=VMEM_LIMIT_BYTES),
        name="mixer",
    )(sinks, x, sbq, sbkt, sbv, swq, swk, swv, tri, gsb, gsw, wout)


def _mlp_kernel(x_ref, g_ref, wup_ref, wdown_ref, gfin_ref, o_ref, *, final_norm):
    x = x_ref[...]
    h = _rms(x, g_ref[...]).astype(BF16)
    acc = x
    for c in range(D_FF // FF_CHUNK):
        u = jnp.dot(h, wup_ref[:, c * FF_CHUNK:(c + 1) * FF_CHUNK],
                    preferred_element_type=F32)
        u = jnp.square(jnp.maximum(u, 0.0)).astype(BF16)
        acc = acc + jnp.dot(u, wdown_ref[c * FF_CHUNK:(c + 1) * FF_CHUNK, :],
                            preferred_element_type=F32)
    if final_norm:
        acc = _rms(acc, gfin_ref[...])
    o_ref[...] = acc


def _mlp(x2d, g, wup, wdown, gfin, final_norm):
    n, d = x2d.shape
    t = MLP_TILE
    const = lambda ti: (0, 0)
    return pl.pallas_call(
        functools.partial(_mlp_kernel, final_norm=final_norm),
        grid=(n // t,),
        in_specs=[
            pl.BlockSpec((t, d), lambda ti: (ti, 0)),
            pl.BlockSpec((1, d), const),
            pl.BlockSpec(wup.shape, const),
            pl.BlockSpec(wdown.shape, const),
            pl.BlockSpec((1, d), const),
        ],
        out_specs=pl.BlockSpec((t, d), lambda ti: (ti, 0)),
        out_shape=jax.ShapeDtypeStruct((n, d), F32),
        compiler_params=pltpu.CompilerParams(
            dimension_semantics=("arbitrary",),
            vmem_limit_bytes=VMEM_LIMIT_BYTES),
        name="mlp",
    )(x2d, g, wup, wdown, gfin)


def kernel(x, positions, attn_norm, w_in, sb_norm, swa_norm, sinks, w_out, mlp_norm, w_up,
           w_down, final_norm):
    b, s, d = x.shape
    depth = w_in.shape[0]
    scale = 1.0 / math.sqrt(HEAD_DIM)
    cos, sin = _rope_tables(positions)

    r = lax.broadcasted_iota(jnp.int32, (BLOCK, BLOCK), 0)
    c = lax.broadcasted_iota(jnp.int32, (BLOCK, BLOCK), 1)
    tri = jnp.concatenate([jnp.where(r >= c, -1.0, 0.0), jnp.full((BLOCK, BLOCK), -1.0)],
                          axis=1).astype(BF16)
    tri = jnp.concatenate([tri, tri], axis=0)

    o = np_cumsum_splits()
    for l in range(depth):
        w = w_in[l]
        wq = (w[:, o[0]:o[1]] * scale).astype(BF16)
        wkt = w[:, o[1]:o[2]].T.astype(BF16)
        wv = w[:, o[2]:o[3]].astype(BF16)
        wsq = (w[:, o[3]:o[4]] * scale).astype(BF16)
        wsk = w[:, o[4]:o[5]].astype(BF16)
        wsv = w[:, o[5]:o[6]].astype(BF16)
        sbq, sbkt, sbv, swq, swk, swv = _in_proj(
            x, attn_norm[l].reshape(1, d), cos, sin, wq, wkt, wv, wsq, wsk, wsv)
        x = _mixer(x, sinks[l], sbq, sbkt, sbv, swq, swk, swv, tri,
                   sb_norm[l].reshape(1, SB_W), swa_norm[l].reshape(1, SWA_Q_W),
                   w_out[l].astype(BF16))
        x = _mlp(x.reshape(b * s, d), mlp_norm[l].reshape(1, d), w_up[l].astype(BF16),
                 w_down[l].astype(BF16), final_norm.reshape(1, d),
                 final_norm=(l == depth - 1)).reshape(b, s, d)
    return x


def np_cumsum_splits():
    widths = [SB_W, SB_W, SB_W, SWA_Q_W, SWA_KV_W, SWA_KV_W]
    offs = [0]
    for w in widths:
        offs.append(offs[-1] + w)
    return offs
```

```python
import functools
import math

import jax
import jax.numpy as jnp
from jax import lax
from jax.experimental import pallas as pl
from jax.experimental.pallas import tpu as pltpu

D_MODEL = 1024
HEAD_DIM = 64
HALF = HEAD_DIM // 2
SB_HEADS = 8
SWA_HEADS = 8
SWA_KV_HEADS = 2
SWA_GROUP = SWA_HEADS // SWA_KV_HEADS
BLOCK = 128
D_FF = 4 * D_MODEL
ROPE_THETA = 10000.0
EPS = 1e-6
NEG = -1e30
EXP_ZERO_BELOW = -105.0

SB_PAIRS = SB_HEADS // 2
SB_UNROLLED_TILES = 3
SB_W = SB_HEADS * HEAD_DIM
SWA_Q_W = SWA_HEADS * HEAD_DIM
SWA_KV_W = SWA_KV_HEADS * HEAD_DIM
SWA_VBD_W = 2 * SWA_KV_HEADS * 2 * HEAD_DIM
MIX_W = SB_W + SWA_Q_W

LANES = 128
VMEM_LIMIT_BYTES = 56 * 1024 * 1024

PROJ_TILE = 512
MLP_TILE = 512
FF_CHUNK = 1024

BF16 = jnp.bfloat16
F32 = jnp.float32
NT_DIMS = (((1,), (1,)), ((), ()))


def _rms(x, g):
    return x * lax.rsqrt(jnp.mean(x * x, axis=-1, keepdims=True) + EPS) * g


def _rope_table_kernel(pos_col_ref, pos_row_ref, freq_row_ref, freq_col_ref,
                       cos_ref, sin_ref, cos_t_ref, sin_t_ref):
    ang = pos_col_ref[...].astype(F32) * freq_row_ref[...]
    lane = lax.broadcasted_iota(jnp.int32, ang.shape, 1)
    first_half = (lane % HEAD_DIM) < HALF
    cos_ref[...] = jnp.cos(ang)
    s = jnp.sin(ang)
    sin_ref[...] = jnp.where(first_half, -s, s)
    ang_t = freq_col_ref[...] * pos_row_ref[...].astype(F32)
    cos_t_ref[...] = jnp.cos(ang_t)
    sin_t_ref[...] = jnp.sin(ang_t)


def _rope_tables(positions):
    s = positions.shape[0]
    inv_freq = 1.0 / (ROPE_THETA ** (jnp.arange(HALF, dtype=F32) * (2.0 / HEAD_DIM)))
    freq_row = jnp.tile(inv_freq, LANES // HALF).reshape(1, LANES)
    return pl.pallas_call(
        _rope_table_kernel,
        out_shape=(jax.ShapeDtypeStruct((s, LANES), F32),
                   jax.ShapeDtypeStruct((s, LANES), F32),
                   jax.ShapeDtypeStruct((HALF, s), F32),
                   jax.ShapeDtypeStruct((HALF, s), F32)),
        name="rope_tables",
    )(positions.reshape(s, 1), positions.reshape(1, s), freq_row, inv_freq.reshape(HALF, 1))


def _rope(x, cos, sin_signed):
    lane = lax.broadcasted_iota(jnp.int32, (x.shape[0], LANES), 1)
    first_half = (lane % HEAD_DIM) < HALF
    outs = []
    for c in range(x.shape[1] // LANES):
        xc = x[:, c * LANES:(c + 1) * LANES]
        swapped = jnp.where(first_half,
                            pltpu.roll(xc, LANES - HALF, axis=1),
                            pltpu.roll(xc, HALF, axis=1))
        outs.append(xc * cos + swapped * sin_signed)
    return jnp.concatenate(outs, axis=1) if len(outs) > 1 else outs[0]


def _rope_t(xt, cos_t, sin_t):
    outs = []
    for h in range(xt.shape[0] // HEAD_DIM):
        x1 = xt[h * HEAD_DIM:h * HEAD_DIM + HALF]
        x2 = xt[h * HEAD_DIM + HALF:(h + 1) * HEAD_DIM]
        outs += [x1 * cos_t - x2 * sin_t, x2 * cos_t + x1 * sin_t]
    return jnp.concatenate(outs, axis=0)


def _in_proj_kernel(x_ref, g_ref, cos_ref, sin_ref, cos_t_ref, sin_t_ref,
                    wq_ref, wkt_ref, wv_ref, wsq_ref, wskt_ref, wsv_ref,
                    sbq_ref, sbkt_ref, sbv_lo_ref, sbv_hi_ref, swq_ref, swkt_ref, swvbd_ref):
    h = _rms(x_ref[0], g_ref[...]).astype(BF16)
    t = h.shape[0]
    dot = functools.partial(jnp.dot, preferred_element_type=F32)
    dot_nt = functools.partial(lax.dot_general, dimension_numbers=NT_DIMS,
                               preferred_element_type=F32)
    sbq_ref[0] = dot(h, wq_ref[...]).astype(BF16)
    sbkt_ref[0] = dot_nt(wkt_ref[...], h).astype(BF16)
    v = dot(h, wv_ref[...])
    even_head = (lax.broadcasted_iota(jnp.int32, (t, SB_W), 1) // HEAD_DIM) % 2 == 0
    sbv_lo_ref[0] = jnp.where(even_head, v, 0.0).astype(BF16)
    sbv_hi_ref[0] = jnp.where(even_head, 0.0, v).astype(BF16)
    swq_ref[0] = _rope(dot(h, wsq_ref[...]), cos_ref[...], sin_ref[...]).astype(BF16)
    swkt_ref[0] = _rope_t(dot_nt(wskt_ref[...], h), cos_t_ref[...], sin_t_ref[...]).astype(BF16)
    sv = dot(h, wsv_ref[...])
    sv_swapped = pltpu.roll(sv, HEAD_DIM, axis=1)
    lo = lax.broadcasted_iota(jnp.int32, (t, LANES), 1) < HEAD_DIM
    swvbd_ref[0] = jnp.concatenate(
        [jnp.where(lo, sv, 0.0), jnp.where(lo, 0.0, sv_swapped),
         jnp.where(lo, sv_swapped, 0.0), jnp.where(lo, 0.0, sv)], axis=1).astype(BF16)


def _in_proj(x, g, tables, wq, wkt, wv, wsq, wskt, wsv):
    b, s, d = x.shape
    t = PROJ_TILE
    cos, sin, cos_t, sin_t = tables
    const = lambda bi, ti: (0, 0)
    tok = lambda w: pl.BlockSpec((1, t, w), lambda bi, ti: (bi, ti, 0))
    feat = lambda w: pl.BlockSpec((1, w, t), lambda bi, ti: (bi, 0, ti))
    out_shape = (
        jax.ShapeDtypeStruct((b, s, SB_W), BF16),
        jax.ShapeDtypeStruct((b, SB_W, s), BF16),
        jax.ShapeDtypeStruct((b, s, SB_W), BF16),
        jax.ShapeDtypeStruct((b, s, SB_W), BF16),
        jax.ShapeDtypeStruct((b, s, SWA_Q_W), BF16),
        jax.ShapeDtypeStruct((b, SWA_KV_W, s), BF16),
        jax.ShapeDtypeStruct((b, s, SWA_VBD_W), BF16),
    )
    return pl.pallas_call(
        _in_proj_kernel,
        grid=(b, s // t),
        in_specs=[
            tok(d),
            pl.BlockSpec((1, d), const),
            pl.BlockSpec((t, LANES), lambda bi, ti: (ti, 0)),
            pl.BlockSpec((t, LANES), lambda bi, ti: (ti, 0)),
            pl.BlockSpec((HALF, t), lambda bi, ti: (0, ti)),
            pl.BlockSpec((HALF, t), lambda bi, ti: (0, ti)),
            pl.BlockSpec(wq.shape, const),
            pl.BlockSpec(wkt.shape, const),
            pl.BlockSpec(wv.shape, const),
            pl.BlockSpec(wsq.shape, const),
            pl.BlockSpec(wskt.shape, const),
            pl.BlockSpec(wsv.shape, const),
        ],
        out_specs=(tok(SB_W), feat(SB_W), tok(SB_W), tok(SB_W),
                   tok(SWA_Q_W), feat(SWA_KV_W), tok(SWA_VBD_W)),
        out_shape=out_shape,
        compiler_params=pltpu.CompilerParams(
            dimension_semantics=("arbitrary", "arbitrary"),
            vmem_limit_bytes=VMEM_LIMIT_BYTES),
        name="in_proj",
    )(x, g, cos, sin, cos_t, sin_t, wq, wkt, wv, wsq, wskt, wsv)


def _block_diag_keys(kt0, kt1):
    z = jnp.zeros_like(kt0)
    return jnp.concatenate([jnp.concatenate([kt0, z], axis=1),
                            jnp.concatenate([z, kt1], axis=1)], axis=0)


def _sb_scores(q, kt_ref, tri, k0, strict=None):
    zs = []
    for p in range(SB_PAIRS):
        kt = kt_ref[0, p * LANES:(p + 1) * LANES, pl.ds(k0, BLOCK)]
        zp = jnp.dot(q[p], _block_diag_keys(kt[:HEAD_DIM], kt[HEAD_DIM:]),
                     preferred_element_type=F32)
        zs += [zp[:, :BLOCK], zp[:, BLOCK:]]
    z = jnp.stack(zs)
    neg_abs = lax.bitcast_convert_type(
        lax.bitcast_convert_type(z, jnp.uint32) | jnp.uint32(0x80000000), F32)
    sp = jnp.maximum(z, 0.0) + jnp.log(1.0 + jnp.exp(neg_abs))
    if strict is not None:
        sp = jnp.where(strict, sp, 0.0)
    sp = sp.reshape(SB_HEADS * BLOCK, BLOCK)
    hi = sp.astype(BF16)
    lo = (sp - hi.astype(F32)).astype(BF16)
    cs = jnp.dot(jnp.concatenate([hi, lo], axis=1), tri, preferred_element_type=F32)
    incl = cs[:, :BLOCK].reshape(SB_HEADS, BLOCK, BLOCK)
    total = cs[:, BLOCK:].reshape(SB_HEADS, BLOCK, BLOCK)
    return z, incl, total


def _sb_values(vlo_ref, vhi_ref, k0, z, incl, strict=None, run=None, valid=None):
    arg = z + incl
    if run is not None:
        arg = arg + run
    a = jnp.exp(arg)
    if strict is not None:
        a = jnp.where(strict, a, 0.0)
    a = a.astype(BF16)
    out = []
    for p in range(SB_PAIRS):
        cols = slice(p * LANES, (p + 1) * LANES)
        vbd = jnp.concatenate([vlo_ref[0, pl.ds(k0, BLOCK), cols],
                               vhi_ref[0, pl.ds(k0, BLOCK), cols]], axis=0)
        if valid is not None:
            vbd = jnp.where(valid, vbd, jnp.zeros_like(vbd))
        a_pair = jnp.concatenate([a[2 * p], a[2 * p + 1]], axis=1)
        out.append(jnp.dot(a_pair, vbd, preferred_element_type=F32))
    return out


def _swa(i, sinks_ref, q_ref, kt_ref, vbd_ref, ones_bd):
    cur0 = pl.multiple_of(i * BLOCK, BLOCK)
    prev0 = pl.multiple_of(jnp.maximum(i - 1, 0) * BLOCK, BLOCK)
    qpos = lax.broadcasted_iota(jnp.int32, (BLOCK, BLOCK), 0)
    kpos = lax.broadcasted_iota(jnp.int32, (BLOCK, BLOCK), 1)
    m_cur = kpos <= qpos
    m_prev = (kpos > qpos) & (i > 0)
    sink_slot = kpos == 0
    keep_prev = m_prev | sink_slot
    first_key = lax.broadcasted_iota(jnp.int32, (2 * BLOCK, LANES), 0) % BLOCK == 0
    pairs_per_kv = SWA_GROUP // 2
    s_cur, s_prev, vbd = [], [], []
    for g in range(SWA_KV_HEADS):
        rows = slice(g * HEAD_DIM, (g + 1) * HEAD_DIM)
        kt_cur = kt_ref[0, rows, pl.ds(cur0, BLOCK)]
        kt_prev = kt_ref[0, rows, pl.ds(prev0, BLOCK)]
        kbd_cur = _block_diag_keys(kt_cur, kt_cur)
        kbd_prev = _block_diag_keys(kt_prev, kt_prev)
        for pp in range(pairs_per_kv):
            p = g * pairs_per_kv + pp
            q_pair = q_ref[0, :, p * LANES:(p + 1) * LANES]
            sc = jnp.dot(q_pair, kbd_cur, preferred_element_type=F32)
            sp = jnp.dot(q_pair, kbd_prev, preferred_element_type=F32)
            s_cur += [sc[:, :BLOCK], sc[:, BLOCK:]]
            s_prev += [sp[:, :BLOCK], sp[:, BLOCK:]]
        cols = slice(2 * g * LANES, 2 * (g + 1) * LANES)
        v_cur = vbd_ref[0, pl.ds(cur0, BLOCK), cols]
        v_prev = vbd_ref[0, pl.ds(prev0, BLOCK), cols]
        v_cur = jnp.concatenate([v_cur[:, :LANES], v_cur[:, LANES:]], axis=0)
        v_prev = jnp.concatenate([v_prev[:, :LANES], v_prev[:, LANES:]], axis=0)
        v_prev = jnp.where(first_key, jnp.zeros_like(v_prev), v_prev)
        vbd.append((jnp.concatenate([v_cur, ones_bd], axis=1),
                    jnp.concatenate([v_prev, ones_bd], axis=1)))
    fill = jnp.stack([jnp.where(sink_slot, sinks_ref[h], NEG) for h in range(SWA_HEADS)])
    sc = jnp.where(m_cur[None], jnp.stack(s_cur), NEG)
    sp = jnp.where(m_prev[None], jnp.stack(s_prev), fill)
    m = jnp.max(jnp.maximum(sc, sp), axis=2, keepdims=True)
    p_cur = jnp.where(m_cur[None], jnp.exp(sc - m), 0.0).astype(BF16)
    p_prev = jnp.where(keep_prev[None], jnp.exp(sp - m), 0.0).astype(BF16)
    outs = []
    for p in range(SWA_HEADS // 2):
        v_cur, v_prev = vbd[p // pairs_per_kv]
        a_cur = jnp.concatenate([p_cur[2 * p], p_cur[2 * p + 1]], axis=1)
        a_prev = jnp.concatenate([p_prev[2 * p], p_prev[2 * p + 1]], axis=1)
        o = (jnp.dot(a_cur, v_cur, preferred_element_type=F32)
             + jnp.dot(a_prev, v_prev, preferred_element_type=F32))
        outs.append(o[:, :LANES] / o[:, LANES:])
    return jnp.concatenate(outs, axis=1)


def _mixer_kernel(sinks_ref, sbq_ref, sbkt_ref, sbv_lo_ref, sbv_hi_ref,
                  swq_ref, swkt_ref, swvbd_ref, tri_ref, ones_ref, gsb_ref, gsw_ref,
                  o_ref, acc_ref, run_ref):
    i = pl.program_id(1)
    tri = tri_ref[...]
    q = [sbq_ref[0, :, p * LANES:(p + 1) * LANES] for p in range(SB_PAIRS)]
    scores = functools.partial(_sb_scores, q, sbkt_ref, tri)
    values = functools.partial(_sb_values, sbv_lo_ref, sbv_hi_ref)

    sw_o = _swa(i, sinks_ref, swq_ref, swkt_ref, swvbd_ref, ones_ref[...])
    o_ref[0, :, SB_W:] = _rms(sw_o, gsw_ref[...]).astype(BF16)

    diag = (lax.broadcasted_iota(jnp.int32, (BLOCK, BLOCK), 1)
            < lax.broadcasted_iota(jnp.int32, (BLOCK, BLOCK), 0))[None]
    k0s = [pl.multiple_of(jnp.maximum(i - t, 0) * BLOCK, BLOCK)
           for t in range(SB_UNROLLED_TILES)]
    front = [scores(k0s[t], strict=diag if t == 0 else None)
             for t in range(SB_UNROLLED_TILES)]
    acc, run = None, None
    for t, (z, incl, total) in enumerate(front):
        out = values(k0s[t], z, incl, strict=diag if t == 0 else None, run=run,
                     valid=None if t == 0 else i - t >= 0)
        acc = out if acc is None else [a + o for a, o in zip(acc, out)]
        run = total if run is None else run + total
    acc_ref[...] = jnp.concatenate(acc, axis=1)
    run_ref[...] = run

    def cond(carry):
        return carry[1]

    def body(carry):
        j, _ = carry
        k0 = pl.multiple_of(j * BLOCK, BLOCK)
        z, incl, total = scores(k0)
        out = values(k0, z, incl, run=run_ref[...])
        acc_ref[...] += jnp.concatenate(out, axis=1)
        new_run = run_ref[...] + total
        run_ref[...] = new_run
        return j - 1, (j > 0) & (jnp.max(new_run) > EXP_ZERO_BELOW)

    more = (i >= SB_UNROLLED_TILES) & (jnp.max(run) > EXP_ZERO_BELOW)
    lax.while_loop(cond, body, (i - SB_UNROLLED_TILES, more))

    o_ref[0, :, :SB_W] = _rms(acc_ref[...], gsb_ref[...]).astype(BF16)


def _mixer(sinks, sbq, sbkt, sbv_lo, sbv_hi, swq, swkt, swvbd, tri, ones_bd, gsb, gsw):
    b, s, _ = sbq.shape
    const = lambda bi, qi: (0, 0)
    resident = lambda a: pl.BlockSpec(a.shape, const, pipeline_mode=pl.Buffered(1))
    qblk = lambda w: pl.BlockSpec((1, BLOCK, w), lambda bi, qi: (bi, qi, 0))
    per_batch = lambda r, c: pl.BlockSpec((1, r, c), lambda bi, qi: (bi, 0, 0))
    return pl.pallas_call(
        _mixer_kernel,
        grid=(b, s // BLOCK),
        in_specs=[
            pl.BlockSpec(memory_space=pltpu.SMEM),
            qblk(SB_W),
            per_batch(SB_W, s),
            per_batch(s, SB_W),
            per_batch(s, SB_W),
            qblk(SWA_Q_W),
            per_batch(SWA_KV_W, s),
            per_batch(s, SWA_VBD_W),
            resident(tri),
            resident(ones_bd),
            resident(gsb),
            resident(gsw),
        ],
        out_specs=qblk(MIX_W),
        out_shape=jax.ShapeDtypeStruct((b, s, MIX_W), BF16),
        scratch_shapes=[pltpu.VMEM((BLOCK, SB_W), F32),
                        pltpu.VMEM((SB_HEADS, BLOCK, BLOCK), F32)],
        compiler_params=pltpu.CompilerParams(
            dimension_semantics=("arbitrary", "arbitrary"),
            vmem_limit_bytes=VMEM_LIMIT_BYTES),
        name="mixer",
    )(sinks, sbq, sbkt, sbv_lo, sbv_hi, swq, swkt, swvbd, tri, ones_bd, gsb, gsw)


def _mlp_kernel(x_ref, mix_ref, wout_ref, g_ref, wup_ref, wdown_ref, gfin_ref, o_ref, *,
                final_norm):
    x = x_ref[...] + jnp.dot(mix_ref[...], wout_ref[...], preferred_element_type=F32)
    h = _rms(x, g_ref[...]).astype(BF16)
    acc = x
    for c in range(D_FF // FF_CHUNK):
        u = jnp.dot(h, wup_ref[:, c * FF_CHUNK:(c + 1) * FF_CHUNK],
                    preferred_element_type=F32)
        u = jnp.square(jnp.maximum(u, 0.0)).astype(BF16)
        acc = acc + jnp.dot(u, wdown_ref[c * FF_CHUNK:(c + 1) * FF_CHUNK, :],
                            preferred_element_type=F32)
    if final_norm:
        acc = _rms(acc, gfin_ref[...])
    o_ref[...] = acc


def _mlp(x2d, mix2d, wout, g, wup, wdown, gfin, final_norm):
    n, d = x2d.shape
    t = MLP_TILE
    const = lambda ti: (0, 0)
    resident = lambda a: pl.BlockSpec(a.shape, const, pipeline_mode=pl.Buffered(1))
    return pl.pallas_call(
        functools.partial(_mlp_kernel, final_norm=final_norm),
        grid=(n // t,),
        in_specs=[
            pl.BlockSpec((t, d), lambda ti: (ti, 0)),
            pl.BlockSpec((t, MIX_W), lambda ti: (ti, 0)),
            resident(wout),
            resident(g),
            resident(wup),
            resident(wdown),
            resident(gfin),
        ],
        out_specs=pl.BlockSpec((t, d), lambda ti: (ti, 0)),
        out_shape=jax.ShapeDtypeStruct((n, d), F32),
        compiler_params=pltpu.CompilerParams(
            dimension_semantics=("arbitrary",),
            vmem_limit_bytes=VMEM_LIMIT_BYTES),
        name="mlp",
    )(x2d, mix2d, wout, g, wup, wdown, gfin)


def _in_proj_offsets():
    widths = [SB_W, SB_W, SB_W, SWA_Q_W, SWA_KV_W, SWA_KV_W]
    offs = [0]
    for w in widths:
        offs.append(offs[-1] + w)
    return offs


def kernel(x, positions, attn_norm, w_in, sb_norm, swa_norm, sinks, w_out, mlp_norm, w_up,
           w_down, final_norm):
    b, s, d = x.shape
    depth = w_in.shape[0]
    scale = 1.0 / math.sqrt(HEAD_DIM)
    tables = _rope_tables(positions)

    r = lax.broadcasted_iota(jnp.int32, (BLOCK, BLOCK), 0)
    c = lax.broadcasted_iota(jnp.int32, (BLOCK, BLOCK), 1)
    tri = jnp.concatenate([jnp.where(r >= c, -1.0, 0.0), jnp.full((BLOCK, BLOCK), -1.0)],
                          axis=1).astype(BF16)
    tri = jnp.concatenate([tri, tri], axis=0)
    rr = lax.broadcasted_iota(jnp.int32, (2 * BLOCK, LANES), 0)
    cc = lax.broadcasted_iota(jnp.int32, (2 * BLOCK, LANES), 1)
    ones_bd = (rr // BLOCK == cc // HEAD_DIM).astype(BF16)

    o = _in_proj_offsets()
    for l in range(depth):
        w = w_in[l]
        wq = (w[:, o[0]:o[1]] * scale).astype(BF16)
        wkt = w[:, o[1]:o[2]].T.astype(BF16)
        wv = w[:, o[2]:o[3]].astype(BF16)
        wsq = (w[:, o[3]:o[4]] * scale).astype(BF16)
        wskt = w[:, o[4]:o[5]].T.astype(BF16)
        wsv = w[:, o[5]:o[6]].astype(BF16)
        sbq, sbkt, sbv_lo, sbv_hi, swq, swkt, swvbd = _in_proj(
            x, attn_norm[l].reshape(1, d), tables, wq, wkt, wv, wsq, wskt, wsv)
        mix = _mixer(sinks[l], sbq, sbkt, sbv_lo, sbv_hi, swq, swkt, swvbd, tri, ones_bd,
                     sb_norm[l].reshape(1, SB_W), swa_norm[l].reshape(1, SWA_Q_W))
        x = _mlp(x.reshape(b * s, d), mix.reshape(b * s, MIX_W), w_out[l].astype(BF16),
                 mlp_norm[l].reshape(1, d), w_up[l].astype(BF16), w_down[l].astype(BF16),
                 final_norm.reshape(1, d), final_norm=(l == depth - 1)).reshape(b, s, d)
    return x
```

```python
import functools
import math

import jax
import jax.numpy as jnp
from jax import lax
from jax.experimental import pallas as pl
from jax.experimental.pallas import tpu as pltpu

D_MODEL = 1024
HEAD_DIM = 64
HALF = HEAD_DIM // 2
SB_HEADS = 8
SWA_HEADS = 8
SWA_KV_HEADS = 2
SWA_GROUP = SWA_HEADS // SWA_KV_HEADS
BLOCK = 128
D_FF = 4 * D_MODEL
ROPE_THETA = 10000.0
EPS = 1e-6
NEG = -1e30
EXP_ZERO_BELOW = -105.0
LOG2_E = 1.4426950408889634

SB_PAIRS = SB_HEADS // 2
SB_FAR_ROWS = BLOCK // 2
SB_W = SB_HEADS * HEAD_DIM
SWA_Q_W = SWA_HEADS * HEAD_DIM
SWA_KV_W = SWA_KV_HEADS * HEAD_DIM
SWA_VBD_W = 2 * SWA_KV_HEADS * 2 * HEAD_DIM
MIX_W = SB_W + SWA_Q_W

LANES = 128
VMEM_LIMIT_BYTES = 56 * 1024 * 1024

PROJ_TILE = 512
MLP_TILE = 512
FF_CHUNK = 1024

BF16 = jnp.bfloat16
F32 = jnp.float32
NT_DIMS = (((1,), (1,)), ((), ()))


def _rms(x, g):
    return x * lax.rsqrt(jnp.mean(x * x, axis=-1, keepdims=True) + EPS) * g


def _rope_table_kernel(pos_col_ref, pos_row_ref, freq_row_ref, freq_col_ref,
                       cos_ref, sin_ref, cos_t_ref, sin_t_ref):
    ang = pos_col_ref[...].astype(F32) * freq_row_ref[...]
    lane = lax.broadcasted_iota(jnp.int32, ang.shape, 1)
    first_half = (lane % HEAD_DIM) < HALF
    cos_ref[...] = jnp.cos(ang)
    s = jnp.sin(ang)
    sin_ref[...] = jnp.where(first_half, -s, s)
    ang_t = freq_col_ref[...] * pos_row_ref[...].astype(F32)
    cos_t_ref[...] = jnp.cos(ang_t)
    sin_t_ref[...] = jnp.sin(ang_t)


def _rope_tables(positions):
    s = positions.shape[0]
    inv_freq = 1.0 / (ROPE_THETA ** (jnp.arange(HALF, dtype=F32) * (2.0 / HEAD_DIM)))
    freq_row = jnp.tile(inv_freq, LANES // HALF).reshape(1, LANES)
    return pl.pallas_call(
        _rope_table_kernel,
        out_shape=(jax.ShapeDtypeStruct((s, LANES), F32),
                   jax.ShapeDtypeStruct((s, LANES), F32),
                   jax.ShapeDtypeStruct((HALF, s), F32),
                   jax.ShapeDtypeStruct((HALF, s), F32)),
        name="rope_tables",
    )(positions.reshape(s, 1), positions.reshape(1, s), freq_row, inv_freq.reshape(HALF, 1))


def _rope(x, cos, sin_signed):
    lane = lax.broadcasted_iota(jnp.int32, (x.shape[0], LANES), 1)
    first_half = (lane % HEAD_DIM) < HALF
    outs = []
    for c in range(x.shape[1] // LANES):
        xc = x[:, c * LANES:(c + 1) * LANES]
        swapped = jnp.where(first_half,
                            pltpu.roll(xc, LANES - HALF, axis=1),
                            pltpu.roll(xc, HALF, axis=1))
        outs.append(xc * cos + swapped * sin_signed)
    return jnp.concatenate(outs, axis=1) if len(outs) > 1 else outs[0]


def _rope_t(xt, cos_t, sin_t):
    outs = []
    for h in range(xt.shape[0] // HEAD_DIM):
        x1 = xt[h * HEAD_DIM:h * HEAD_DIM + HALF]
        x2 = xt[h * HEAD_DIM + HALF:(h + 1) * HEAD_DIM]
        outs += [x1 * cos_t - x2 * sin_t, x2 * cos_t + x1 * sin_t]
    return jnp.concatenate(outs, axis=0)


def _in_proj_kernel(x_ref, g_ref, cos_ref, sin_ref, cos_t_ref, sin_t_ref,
                    wq_ref, wkt_ref, wv_ref, wsq_ref, wskt_ref, wsv_ref,
                    sbq_ref, sbkt_ref, sbv_lo_ref, sbv_hi_ref, swq_ref, swkt_ref, swvbd_ref):
    h = _rms(x_ref[0], g_ref[...]).astype(BF16)
    t = h.shape[0]
    dot = functools.partial(jnp.dot, preferred_element_type=F32)
    dot_nt = functools.partial(lax.dot_general, dimension_numbers=NT_DIMS,
                               preferred_element_type=F32)
    sbq_ref[0] = dot(h, wq_ref[...]).astype(BF16)
    sbkt_ref[0] = dot_nt(wkt_ref[...], h).astype(BF16)
    v = dot(h, wv_ref[...])
    even_head = (lax.broadcasted_iota(jnp.int32, (t, SB_W), 1) // HEAD_DIM) % 2 == 0
    sbv_lo_ref[0] = jnp.where(even_head, v, 0.0).astype(BF16)
    sbv_hi_ref[0] = jnp.where(even_head, 0.0, v).astype(BF16)
    swq_ref[0] = _rope(dot(h, wsq_ref[...]), cos_ref[...], sin_ref[...]).astype(BF16)
    swkt_ref[0] = _rope_t(dot_nt(wskt_ref[...], h), cos_t_ref[...], sin_t_ref[...]).astype(BF16)
    sv = dot(h, wsv_ref[...])
    sv_swapped = pltpu.roll(sv, HEAD_DIM, axis=1)
    lo = lax.broadcasted_iota(jnp.int32, (t, LANES), 1) < HEAD_DIM
    swvbd_ref[0] = jnp.concatenate(
        [jnp.where(lo, sv, 0.0), jnp.where(lo, 0.0, sv_swapped),
         jnp.where(lo, sv_swapped, 0.0), jnp.where(lo, 0.0, sv)], axis=1).astype(BF16)


def _in_proj(x, g, tables, wq, wkt, wv, wsq, wskt, wsv):
    b, s, d = x.shape
    t = PROJ_TILE
    cos, sin, cos_t, sin_t = tables
    const = lambda bi, ti: (0, 0)
    tok = lambda w: pl.BlockSpec((1, t, w), lambda bi, ti: (bi, ti, 0))
    feat = lambda w: pl.BlockSpec((1, w, t), lambda bi, ti: (bi, 0, ti))
    out_shape = (
        jax.ShapeDtypeStruct((b, s, SB_W), BF16),
        jax.ShapeDtypeStruct((b, SB_W, s), BF16),
        jax.ShapeDtypeStruct((b, s, SB_W), BF16),
        jax.ShapeDtypeStruct((b, s, SB_W), BF16),
        jax.ShapeDtypeStruct((b, s, SWA_Q_W), BF16),
        jax.ShapeDtypeStruct((b, SWA_KV_W, s), BF16),
        jax.ShapeDtypeStruct((b, s, SWA_VBD_W), BF16),
    )
    return pl.pallas_call(
        _in_proj_kernel,
        grid=(b, s // t),
        in_specs=[
            tok(d),
            pl.BlockSpec((1, d), const),
            pl.BlockSpec((t, LANES), lambda bi, ti: (ti, 0)),
            pl.BlockSpec((t, LANES), lambda bi, ti: (ti, 0)),
            pl.BlockSpec((HALF, t), lambda bi, ti: (0, ti)),
            pl.BlockSpec((HALF, t), lambda bi, ti: (0, ti)),
            pl.BlockSpec(wq.shape, const),
            pl.BlockSpec(wkt.shape, const),
            pl.BlockSpec(wv.shape, const),
            pl.BlockSpec(wsq.shape, const),
            pl.BlockSpec(wskt.shape, const),
            pl.BlockSpec(wsv.shape, const),
        ],
        out_specs=(tok(SB_W), feat(SB_W), tok(SB_W), tok(SB_W),
                   tok(SWA_Q_W), feat(SWA_KV_W), tok(SWA_VBD_W)),
        out_shape=out_shape,
        compiler_params=pltpu.CompilerParams(
            dimension_semantics=("arbitrary", "arbitrary"),
            vmem_limit_bytes=VMEM_LIMIT_BYTES),
        name="in_proj",
    )(x, g, cos, sin, cos_t, sin_t, wq, wkt, wv, wsq, wskt, wsv)


def _block_diag_keys(kt0, kt1):
    z = jnp.zeros_like(kt0)
    return jnp.concatenate([jnp.concatenate([kt0, z], axis=1),
                            jnp.concatenate([z, kt1], axis=1)], axis=0)


def _sb_scores(q, kt_ref, tri, k0, rows, strict=None, run=None):
    zs = []
    for p in range(SB_PAIRS):
        kt = kt_ref[0, p * LANES:(p + 1) * LANES, pl.ds(k0, BLOCK)]
        zp = jnp.dot(q[p][:rows], _block_diag_keys(kt[:HEAD_DIM], kt[HEAD_DIM:]),
                     preferred_element_type=F32)
        zs += [zp[:, :BLOCK], zp[:, BLOCK:]]
    z = jnp.stack(zs)
    sp = jnp.maximum(z, 0.0) + jnp.log(1.0 + jnp.exp2(jnp.abs(z) * (-LOG2_E)))
    if strict is not None:
        sp = jnp.where(strict, sp, 0.0)
    sp = sp.reshape(SB_HEADS * rows, BLOCK)
    hi = sp.astype(BF16)
    lo = (sp - hi.astype(F32)).astype(BF16)
    cs = jnp.dot(jnp.concatenate([hi, lo], axis=1), tri, preferred_element_type=F32)
    if run is not None:
        run2d = run.reshape(SB_HEADS * rows, BLOCK)
        cs = cs + jnp.concatenate([run2d, run2d], axis=1)
    incl = cs[:, :BLOCK].reshape(SB_HEADS, rows, BLOCK)
    total = cs[:, BLOCK:].reshape(SB_HEADS, rows, BLOCK)
    return z, incl, total


def _sb_values(vlo_ref, vhi_ref, k0, z, incl, mask=None, valid=None):
    a = jnp.exp(z + incl)
    if mask is not None:
        a = jnp.where(mask, a, 0.0)
    a = a.astype(BF16)
    out = []
    for p in range(SB_PAIRS):
        cols = slice(p * LANES, (p + 1) * LANES)
        vbd = jnp.concatenate([vlo_ref[0, pl.ds(k0, BLOCK), cols],
                               vhi_ref[0, pl.ds(k0, BLOCK), cols]], axis=0)
        if valid is not None:
            vbd = jnp.where(valid, vbd, jnp.zeros_like(vbd))
        a_pair = jnp.concatenate([a[2 * p], a[2 * p + 1]], axis=1)
        out.append(jnp.dot(a_pair, vbd, preferred_element_type=F32))
    return out


def _swa(i, sinks_ref, q_ref, kt_ref, vbd_ref, ones_bd):
    cur0 = pl.multiple_of(i * BLOCK, BLOCK)
    prev0 = pl.multiple_of(jnp.maximum(i - 1, 0) * BLOCK, BLOCK)
    qpos = lax.broadcasted_iota(jnp.int32, (BLOCK, BLOCK), 0)
    kpos = lax.broadcasted_iota(jnp.int32, (BLOCK, BLOCK), 1)
    m_cur = kpos <= qpos
    m_prev = (kpos > qpos) & (i > 0)
    sink_slot = kpos == 0
    keep_prev = m_prev | sink_slot
    first_key = lax.broadcasted_iota(jnp.int32, (2 * BLOCK, LANES), 0) % BLOCK == 0
    pairs_per_kv = SWA_GROUP // 2
    s_cur, s_prev, vbd = [], [], []
    for g in range(SWA_KV_HEADS):
        rows = slice(g * HEAD_DIM, (g + 1) * HEAD_DIM)
        kt_cur = kt_ref[0, rows, pl.ds(cur0, BLOCK)]
        kt_prev = kt_ref[0, rows, pl.ds(prev0, BLOCK)]
        kbd_cur = _block_diag_keys(kt_cur, kt_cur)
        kbd_prev = _block_diag_keys(kt_prev, kt_prev)
        for pp in range(pairs_per_kv):
            p = g * pairs_per_kv + pp
            q_pair = q_ref[0, :, p * LANES:(p + 1) * LANES]
            sc = jnp.dot(q_pair, kbd_cur, preferred_element_type=F32)
            sp = jnp.dot(q_pair, kbd_prev, preferred_element_type=F32)
            s_cur += [sc[:, :BLOCK], sc[:, BLOCK:]]
            s_prev += [sp[:, :BLOCK], sp[:, BLOCK:]]
        cols = slice(2 * g * LANES, 2 * (g + 1) * LANES)
        v_cur = vbd_ref[0, pl.ds(cur0, BLOCK), cols]
        v_prev = vbd_ref[0, pl.ds(prev0, BLOCK), cols]
        v_cur = jnp.concatenate([v_cur[:, :LANES], v_cur[:, LANES:]], axis=0)
        v_prev = jnp.concatenate([v_prev[:, :LANES], v_prev[:, LANES:]], axis=0)
        v_prev = jnp.where(first_key, jnp.zeros_like(v_prev), v_prev)
        vbd.append((jnp.concatenate([v_cur, ones_bd], axis=1),
                    jnp.concatenate([v_prev, ones_bd], axis=1)))
    fill = jnp.stack([jnp.where(sink_slot, sinks_ref[h], NEG) for h in range(SWA_HEADS)])
    sc = jnp.where(m_cur[None], jnp.stack(s_cur), NEG)
    sp = jnp.where(m_prev[None], jnp.stack(s_prev), fill)
    m = jnp.max(jnp.maximum(sc, sp), axis=2, keepdims=True)
    p_cur = jnp.where(m_cur[None], jnp.exp(sc - m), 0.0).astype(BF16)
    p_prev = jnp.where(keep_prev[None], jnp.exp(sp - m), 0.0).astype(BF16)
    outs = []
    for p in range(SWA_HEADS // 2):
        v_cur, v_prev = vbd[p // pairs_per_kv]
        a_cur = jnp.concatenate([p_cur[2 * p], p_cur[2 * p + 1]], axis=1)
        a_prev = jnp.concatenate([p_prev[2 * p], p_prev[2 * p + 1]], axis=1)
        o = (jnp.dot(a_cur, v_cur, preferred_element_type=F32)
             + jnp.dot(a_prev, v_prev, preferred_element_type=F32))
        outs.append(o[:, :LANES] / o[:, LANES:])
    return jnp.concatenate(outs, axis=1)


def _mixer_kernel(sinks_ref, sbq_ref, sbkt_ref, sbv_lo_ref, sbv_hi_ref,
                  swq_ref, swkt_ref, swvbd_ref, tri_ref, ones_ref, gsb_ref, gsw_ref,
                  o_ref, acc_ref, run_ref):
    i = pl.program_id(1)
    tri = tri_ref[...]
    q = [sbq_ref[0, :, p * LANES:(p + 1) * LANES] for p in range(SB_PAIRS)]
    scores = functools.partial(_sb_scores, q, sbkt_ref, tri)
    values = functools.partial(_sb_values, sbv_lo_ref, sbv_hi_ref)

    sw_o = _swa(i, sinks_ref, swq_ref, swkt_ref, swvbd_ref, ones_ref[...])
    o_ref[0, :, SB_W:] = _rms(sw_o, gsw_ref[...]).astype(BF16)

    qrow = lax.broadcasted_iota(jnp.int32, (BLOCK, BLOCK), 0)
    diag = (lax.broadcasted_iota(jnp.int32, (BLOCK, BLOCK), 1) < qrow)[None]
    k0s = [pl.multiple_of(jnp.maximum(i - t, 0) * BLOCK, BLOCK) for t in range(3)]
    z0, c0, run0 = scores(k0s[0], BLOCK, strict=diag)
    z1, c1, run1 = scores(k0s[1], BLOCK, run=run0)
    z2, c2, run2 = scores(k0s[2], SB_FAR_ROWS, run=run1[:, :SB_FAR_ROWS])
    more = (i >= 2) & ((jnp.max(run1[:, SB_FAR_ROWS:]) > EXP_ZERO_BELOW)
                       | ((i >= 3) & (jnp.max(run2) > EXP_ZERO_BELOW)))
    out0 = values(k0s[0], z0, c0, mask=diag)
    out1 = values(k0s[1], z1, c1, valid=i >= 1)
    out2 = values(k0s[2], z2, c2, valid=i >= 2)
    acc = jnp.concatenate([a + b for a, b in zip(out0, out1)], axis=1)
    acc = jnp.concatenate([acc[:SB_FAR_ROWS] + jnp.concatenate(out2, axis=1),
                           acc[SB_FAR_ROWS:]], axis=0)
    o_ref[0, :, :SB_W] = _rms(acc, gsb_ref[...]).astype(BF16)

    def cond(carry):
        return carry[1]

    def body(carry):
        j, _ = carry
        k0 = pl.multiple_of(j * BLOCK, BLOCK)
        keep = (qrow >= jnp.where(j == i - 2, SB_FAR_ROWS, 0))[None]
        run = run_ref[...]
        z, incl, total = scores(k0, BLOCK, run=run)
        out = values(k0, z, incl, mask=keep)
        acc_ref[...] += jnp.concatenate(out, axis=1)
        new_run = jnp.where(keep, total, run)
        run_ref[...] = new_run
        return j - 1, (j > 0) & (jnp.max(new_run) > EXP_ZERO_BELOW)

    @pl.when(more)
    def _():
        acc_ref[...] = acc
        run_ref[...] = run1
        run_ref[:, :SB_FAR_ROWS, :] = run2
        lax.while_loop(cond, body, (i - 2, more))
        o_ref[0, :, :SB_W] = _rms(acc_ref[...], gsb_ref[...]).astype(BF16)


def _mixer(sinks, sbq, sbkt, sbv_lo, sbv_hi, swq, swkt, swvbd, tri, ones_bd, gsb, gsw):
    b, s, _ = sbq.shape
    const = lambda bi, qi: (0, 0)
    resident = lambda a: pl.BlockSpec(a.shape, const, pipeline_mode=pl.Buffered(1))
    qblk = lambda w: pl.BlockSpec((1, BLOCK, w), lambda bi, qi: (bi, qi, 0))
    per_batch = lambda r, c: pl.BlockSpec((1, r, c), lambda bi, qi: (bi, 0, 0))
    return pl.pallas_call(
        _mixer_kernel,
        grid=(b, s // BLOCK),
        in_specs=[
            pl.BlockSpec(memory_space=pltpu.SMEM),
            qblk(SB_W),
            per_batch(SB_W, s),
            per_batch(s, SB_W),
            per_batch(s, SB_W),
            qblk(SWA_Q_W),
            per_batch(SWA_KV_W, s),
            per_batch(s, SWA_VBD_W),
            resident(tri),
            resident(ones_bd),
            resident(gsb),
            resident(gsw),
        ],
        out_specs=qblk(MIX_W),
        out_shape=jax.ShapeDtypeStruct((b, s, MIX_W), BF16),
        scratch_shapes=[pltpu.VMEM((BLOCK, SB_W), F32),
                        pltpu.VMEM((SB_HEADS, BLOCK, BLOCK), F32)],
        compiler_params=pltpu.CompilerParams(
            dimension_semantics=("arbitrary", "arbitrary"),
            vmem_limit_bytes=VMEM_LIMIT_BYTES),
        name="mixer",
    )(sinks, sbq, sbkt, sbv_lo, sbv_hi, swq, swkt, swvbd, tri, ones_bd, gsb, gsw)


def _mlp_kernel(x_ref, mix_ref, wout_ref, g_ref, wup_ref, wdown_ref, gfin_ref, o_ref, *,
                final_norm):
    x = x_ref[...] + jnp.dot(mix_ref[...], wout_ref[...], preferred_element_type=F32)
    h = _rms(x, g_ref[...]).astype(BF16)
    acc = x
    for c in range(D_FF // FF_CHUNK):
        u = jnp.dot(h, wup_ref[:, c * FF_CHUNK:(c + 1) * FF_CHUNK],
                    preferred_element_type=F32)
        u = jnp.square(jnp.maximum(u, 0.0)).astype(BF16)
        acc = acc + jnp.dot(u, wdown_ref[c * FF_CHUNK:(c + 1) * FF_CHUNK, :],
                            preferred_element_type=F32)
    if final_norm:
        acc = _rms(acc, gfin_ref[...])
    o_ref[...] = acc


def _mlp(x2d, mix2d, wout, g, wup, wdown, gfin, final_norm):
    n, d = x2d.shape
    t = MLP_TILE
    const = lambda ti: (0, 0)
    resident = lambda a: pl.BlockSpec(a.shape, const, pipeline_mode=pl.Buffered(1))
    return pl.pallas_call(
        functools.partial(_mlp_kernel, final_norm=final_norm),
        grid=(n // t,),
        in_specs=[
            pl.BlockSpec((t, d), lambda ti: (ti, 0)),
            pl.BlockSpec((t, MIX_W), lambda ti: (ti, 0)),
            resident(wout),
            resident(g),
            resident(wup),
            resident(wdown),
            resident(gfin),
        ],
        out_specs=pl.BlockSpec((t, d), lambda ti: (ti, 0)),
        out_shape=jax.ShapeDtypeStruct((n, d), F32),
        compiler_params=pltpu.CompilerParams(
            dimension_semantics=("arbitrary",),
            vmem_limit_bytes=VMEM_LIMIT_BYTES),
        name="mlp",
    )(x2d, mix2d, wout, g, wup, wdown, gfin)


def _in_proj_offsets():
    widths = [SB_W, SB_W, SB_W, SWA_Q_W, SWA_KV_W, SWA_KV_W]
    offs = [0]
    for w in widths:
        offs.append(offs[-1] + w)
    return offs


def kernel(x, positions, attn_norm, w_in, sb_norm, swa_norm, sinks, w_out, mlp_norm, w_up,
           w_down, final_norm):
    b, s, d = x.shape
    depth = w_in.shape[0]
    scale = 1.0 / math.sqrt(HEAD_DIM)
    tables = _rope_tables(positions)

    r = lax.broadcasted_iota(jnp.int32, (BLOCK, BLOCK), 0)
    c = lax.broadcasted_iota(jnp.int32, (BLOCK, BLOCK), 1)
    tri = jnp.concatenate([jnp.where(r >= c, -1.0, 0.0), jnp.full((BLOCK, BLOCK), -1.0)],
                          axis=1).astype(BF16)
    tri = jnp.concatenate([tri, tri], axis=0)
    rr = lax.broadcasted_iota(jnp.int32, (2 * BLOCK, LANES), 0)
    cc = lax.broadcasted_iota(jnp.int32, (2 * BLOCK, LANES), 1)
    ones_bd = (rr // BLOCK == cc // HEAD_DIM).astype(BF16)

    o = _in_proj_offsets()
    for l in range(depth):
        w = w_in[l]
        wq = (w[:, o[0]:o[1]] * scale).astype(BF16)
        wkt = w[:, o[1]:o[2]].T.astype(BF16)
        wv = w[:, o[2]:o[3]].astype(BF16)
        wsq = (w[:, o[3]:o[4]] * scale).astype(BF16)
        wskt = w[:, o[4]:o[5]].T.astype(BF16)
        wsv = w[:, o[5]:o[6]].astype(BF16)
        sbq, sbkt, sbv_lo, sbv_hi, swq, swkt, swvbd = _in_proj(
            x, attn_norm[l].reshape(1, d), tables, wq, wkt, wv, wsq, wskt, wsv)
        mix = _mixer(sinks[l], sbq, sbkt, sbv_lo, sbv_hi, swq, swkt, swvbd, tri, ones_bd,
                     sb_norm[l].reshape(1, SB_W), swa_norm[l].reshape(1, SWA_Q_W))
        x = _mlp(x.reshape(b * s, d), mix.reshape(b * s, MIX_W), w_out[l].astype(BF16),
                 mlp_norm[l].reshape(1, d), w_up[l].astype(BF16), w_down[l].astype(BF16),
                 final_norm.reshape(1, d), final_norm=(l == depth - 1)).reshape(b, s, d)
    return x
```

```python
import functools
import math

import jax
import jax.numpy as jnp
from jax import lax
from jax.experimental import pallas as pl
from jax.experimental.pallas import tpu as pltpu

D_MODEL = 1024
HEAD_DIM = 64
HALF = HEAD_DIM // 2
SB_HEADS = 8
SWA_HEADS = 8
SWA_KV_HEADS = 2
SWA_GROUP = SWA_HEADS // SWA_KV_HEADS
BLOCK = 128
D_FF = 4 * D_MODEL
ROPE_THETA = 10000.0
EPS = 1e-6
NEG = -1e30
EXP_ZERO_BELOW = -105.0
LOG2_E = 1.4426950408889634

SB_PAIRS = SB_HEADS // 2
SB_UNROLLED_TILES = 3
MIXER_BLOCKS = 2
SB_W = SB_HEADS * HEAD_DIM
SWA_Q_W = SWA_HEADS * HEAD_DIM
SWA_KV_W = SWA_KV_HEADS * HEAD_DIM
SWA_VBD_W = 2 * SWA_KV_HEADS * 2 * HEAD_DIM
MIX_W = SB_W + SWA_Q_W

LANES = 128
VMEM_LIMIT_BYTES = 56 * 1024 * 1024

PROJ_TILE = 1024
MLP_TILE = 1024
FF_CHUNK = 1024

BF16 = jnp.bfloat16
F32 = jnp.float32
NT_DIMS = (((1,), (1,)), ((), ()))


def _rms(x, g):
    return x * lax.rsqrt(jnp.mean(x * x, axis=-1, keepdims=True) + EPS) * g


def _rope_table_kernel(pos_col_ref, pos_row_ref, freq_row_ref, freq_col_ref,
                       cos_ref, sin_ref, cos_t_ref, sin_t_ref):
    ang = pos_col_ref[...].astype(F32) * freq_row_ref[...]
    lane = lax.broadcasted_iota(jnp.int32, ang.shape, 1)
    first_half = (lane % HEAD_DIM) < HALF
    cos_ref[...] = jnp.cos(ang)
    s = jnp.sin(ang)
    sin_ref[...] = jnp.where(first_half, -s, s)
    ang_t = freq_col_ref[...] * pos_row_ref[...].astype(F32)
    cos_t_ref[...] = jnp.cos(ang_t)
    sin_t_ref[...] = jnp.sin(ang_t)


def _rope_tables(positions):
    s = positions.shape[0]
    inv_freq = 1.0 / (ROPE_THETA ** (jnp.arange(HALF, dtype=F32) * (2.0 / HEAD_DIM)))
    freq_row = jnp.tile(inv_freq, LANES // HALF).reshape(1, LANES)
    return pl.pallas_call(
        _rope_table_kernel,
        out_shape=(jax.ShapeDtypeStruct((s, LANES), F32),
                   jax.ShapeDtypeStruct((s, LANES), F32),
                   jax.ShapeDtypeStruct((HALF, s), F32),
                   jax.ShapeDtypeStruct((HALF, s), F32)),
        name="rope_tables",
    )(positions.reshape(s, 1), positions.reshape(1, s), freq_row, inv_freq.reshape(HALF, 1))


def _rope(x, cos, sin_signed):
    lane = lax.broadcasted_iota(jnp.int32, (x.shape[0], LANES), 1)
    first_half = (lane % HEAD_DIM) < HALF
    outs = []
    for c in range(x.shape[1] // LANES):
        xc = x[:, c * LANES:(c + 1) * LANES]
        swapped = jnp.where(first_half,
                            pltpu.roll(xc, LANES - HALF, axis=1),
                            pltpu.roll(xc, HALF, axis=1))
        outs.append(xc * cos + swapped * sin_signed)
    return jnp.concatenate(outs, axis=1) if len(outs) > 1 else outs[0]


def _rope_t(xt, cos_t, sin_t):
    outs = []
    for h in range(xt.shape[0] // HEAD_DIM):
        x1 = xt[h * HEAD_DIM:h * HEAD_DIM + HALF]
        x2 = xt[h * HEAD_DIM + HALF:(h + 1) * HEAD_DIM]
        outs += [x1 * cos_t - x2 * sin_t, x2 * cos_t + x1 * sin_t]
    return jnp.concatenate(outs, axis=0)


def _in_proj_kernel(x_ref, g_ref, cos_ref, sin_ref, cos_t_ref, sin_t_ref,
                    wq_ref, wkt_ref, wv_ref, wsq_ref, wskt_ref, wsv_ref,
                    sbq_ref, sbkt_ref, sbv_lo_ref, sbv_hi_ref, swq_ref, swkt_ref, swvbd_ref):
    h = _rms(x_ref[0], g_ref[...]).astype(BF16)
    t = h.shape[0]
    dot = functools.partial(jnp.dot, preferred_element_type=F32)
    dot_nt = functools.partial(lax.dot_general, dimension_numbers=NT_DIMS,
                               preferred_element_type=F32)
    sbq_ref[0] = dot(h, wq_ref[...]).astype(BF16)
    sbkt_ref[0] = dot_nt(wkt_ref[...], h).astype(BF16)
    v = dot(h, wv_ref[...])
    even_head = (lax.broadcasted_iota(jnp.int32, (t, SB_W), 1) // HEAD_DIM) % 2 == 0
    sbv_lo_ref[0] = jnp.where(even_head, v, 0.0).astype(BF16)
    sbv_hi_ref[0] = jnp.where(even_head, 0.0, v).astype(BF16)
    swq_ref[0] = _rope(dot(h, wsq_ref[...]), cos_ref[...], sin_ref[...]).astype(BF16)
    swkt_ref[0] = _rope_t(dot_nt(wskt_ref[...], h), cos_t_ref[...], sin_t_ref[...]).astype(BF16)
    sv = dot(h, wsv_ref[...])
    sv_swapped = pltpu.roll(sv, HEAD_DIM, axis=1)
    lo = lax.broadcasted_iota(jnp.int32, (t, LANES), 1) < HEAD_DIM
    swvbd_ref[0] = jnp.concatenate(
        [jnp.where(lo, sv, 0.0), jnp.where(lo, 0.0, sv_swapped),
         jnp.where(lo, sv_swapped, 0.0), jnp.where(lo, 0.0, sv)], axis=1).astype(BF16)


def _in_proj(x, g, tables, wq, wkt, wv, wsq, wskt, wsv):
    b, s, d = x.shape
    t = PROJ_TILE
    cos, sin, cos_t, sin_t = tables
    const = lambda bi, ti: (0, 0)
    tok = lambda w: pl.BlockSpec((1, t, w), lambda bi, ti: (bi, ti, 0))
    feat = lambda w: pl.BlockSpec((1, w, t), lambda bi, ti: (bi, 0, ti))
    out_shape = (
        jax.ShapeDtypeStruct((b, s, SB_W), BF16),
        jax.ShapeDtypeStruct((b, SB_W, s), BF16),
        jax.ShapeDtypeStruct((b, s, SB_W), BF16),
        jax.ShapeDtypeStruct((b, s, SB_W), BF16),
        jax.ShapeDtypeStruct((b, s, SWA_Q_W), BF16),
        jax.ShapeDtypeStruct((b, SWA_KV_W, s), BF16),
        jax.ShapeDtypeStruct((b, s, SWA_VBD_W), BF16),
    )
    return pl.pallas_call(
        _in_proj_kernel,
        grid=(b, s // t),
        in_specs=[
            tok(d),
            pl.BlockSpec((1, d), const),
            pl.BlockSpec((t, LANES), lambda bi, ti: (ti, 0)),
            pl.BlockSpec((t, LANES), lambda bi, ti: (ti, 0)),
            pl.BlockSpec((HALF, t), lambda bi, ti: (0, ti)),
            pl.BlockSpec((HALF, t), lambda bi, ti: (0, ti)),
            pl.BlockSpec(wq.shape, const),
            pl.BlockSpec(wkt.shape, const),
            pl.BlockSpec(wv.shape, const),
            pl.BlockSpec(wsq.shape, const),
            pl.BlockSpec(wskt.shape, const),
            pl.BlockSpec(wsv.shape, const),
        ],
        out_specs=(tok(SB_W), feat(SB_W), tok(SB_W), tok(SB_W),
                   tok(SWA_Q_W), feat(SWA_KV_W), tok(SWA_VBD_W)),
        out_shape=out_shape,
        compiler_params=pltpu.CompilerParams(
            dimension_semantics=("arbitrary", "arbitrary"),
            vmem_limit_bytes=VMEM_LIMIT_BYTES),
        name="in_proj",
    )(x, g, cos, sin, cos_t, sin_t, wq, wkt, wv, wsq, wskt, wsv)


def _block_diag_keys(kt0, kt1):
    z = jnp.zeros_like(kt0)
    return jnp.concatenate([jnp.concatenate([kt0, z], axis=1),
                            jnp.concatenate([z, kt1], axis=1)], axis=0)


def _sb_scores(q, kt_ref, tri, k0, strict=None, run=None):
    rows = BLOCK
    zs = []
    for p in range(SB_PAIRS):
        kt = kt_ref[0, p * LANES:(p + 1) * LANES, pl.ds(k0, BLOCK)]
        zp = jnp.dot(q[p], _block_diag_keys(kt[:HEAD_DIM], kt[HEAD_DIM:]),
                     preferred_element_type=F32)
        zs += [zp[:, :BLOCK], zp[:, BLOCK:]]
    z = jnp.stack(zs)
    sp = jnp.maximum(z, 0.0) + jnp.log(1.0 + jnp.exp2(jnp.abs(z) * (-LOG2_E)))
    if strict is not None:
        sp = jnp.where(strict, sp, 0.0)
    sp = sp.reshape(SB_HEADS * rows, BLOCK)
    hi = sp.astype(BF16)
    lo = (sp - hi.astype(F32)).astype(BF16)
    cs = jnp.dot(jnp.concatenate([hi, lo], axis=1), tri, preferred_element_type=F32)
    if run is not None:
        run2d = run.reshape(SB_HEADS * rows, BLOCK)
        cs = cs + jnp.concatenate([run2d, run2d], axis=1)
    incl = cs[:, :BLOCK].reshape(SB_HEADS, rows, BLOCK)
    total = cs[:, BLOCK:].reshape(SB_HEADS, rows, BLOCK)
    return z, incl, total


def _sb_values(vlo_ref, vhi_ref, k0, z, incl, mask=None, valid=None):
    a = jnp.exp(z + incl)
    if mask is not None:
        a = jnp.where(mask, a, 0.0)
    a = a.astype(BF16)
    out = []
    for p in range(SB_PAIRS):
        cols = slice(p * LANES, (p + 1) * LANES)
        vbd = jnp.concatenate([vlo_ref[0, pl.ds(k0, BLOCK), cols],
                               vhi_ref[0, pl.ds(k0, BLOCK), cols]], axis=0)
        if valid is not None:
            vbd = jnp.where(valid, vbd, jnp.zeros_like(vbd))
        a_pair = jnp.concatenate([a[2 * p], a[2 * p + 1]], axis=1)
        out.append(jnp.dot(a_pair, vbd, preferred_element_type=F32))
    return out


def _swa(i, qrows, sinks_ref, q_ref, kt_ref, vbd_ref, ones_bd):
    cur0 = pl.multiple_of(i * BLOCK, BLOCK)
    prev0 = pl.multiple_of(jnp.maximum(i - 1, 0) * BLOCK, BLOCK)
    qpos = lax.broadcasted_iota(jnp.int32, (BLOCK, BLOCK), 0)
    kpos = lax.broadcasted_iota(jnp.int32, (BLOCK, BLOCK), 1)
    m_cur = kpos <= qpos
    m_prev = (kpos > qpos) & (i > 0)
    sink_slot = kpos == 0
    keep_prev = m_prev | sink_slot
    first_key = lax.broadcasted_iota(jnp.int32, (2 * BLOCK, LANES), 0) % BLOCK == 0
    pairs_per_kv = SWA_GROUP // 2
    s_cur, s_prev, vbd = [], [], []
    for g in range(SWA_KV_HEADS):
        rows = slice(g * HEAD_DIM, (g + 1) * HEAD_DIM)
        kt_cur = kt_ref[0, rows, pl.ds(cur0, BLOCK)]
        kt_prev = kt_ref[0, rows, pl.ds(prev0, BLOCK)]
        kbd_cur = _block_diag_keys(kt_cur, kt_cur)
        kbd_prev = _block_diag_keys(kt_prev, kt_prev)
        for pp in range(pairs_per_kv):
            p = g * pairs_per_kv + pp
            q_pair = q_ref[0, qrows, p * LANES:(p + 1) * LANES]
            sc = jnp.dot(q_pair, kbd_cur, preferred_element_type=F32)
            sp = jnp.dot(q_pair, kbd_prev, preferred_element_type=F32)
            s_cur += [sc[:, :BLOCK], sc[:, BLOCK:]]
            s_prev += [sp[:, :BLOCK], sp[:, BLOCK:]]
        cols = slice(2 * g * LANES, 2 * (g + 1) * LANES)
        v_cur = vbd_ref[0, pl.ds(cur0, BLOCK), cols]
        v_prev = vbd_ref[0, pl.ds(prev0, BLOCK), cols]
        v_cur = jnp.concatenate([v_cur[:, :LANES], v_cur[:, LANES:]], axis=0)
        v_prev = jnp.concatenate([v_prev[:, :LANES], v_prev[:, LANES:]], axis=0)
        v_prev = jnp.where(first_key, jnp.zeros_like(v_prev), v_prev)
        vbd.append((jnp.concatenate([v_cur, ones_bd], axis=1),
                    jnp.concatenate([v_prev, ones_bd], axis=1)))
    fill = jnp.stack([jnp.where(sink_slot, sinks_ref[h], NEG) for h in range(SWA_HEADS)])
    sc = jnp.where(m_cur[None], jnp.stack(s_cur), NEG)
    sp = jnp.where(m_prev[None], jnp.stack(s_prev), fill)
    m = jnp.max(jnp.maximum(sc, sp), axis=2, keepdims=True)
    p_cur = jnp.where(m_cur[None], jnp.exp(sc - m), 0.0).astype(BF16)
    p_prev = jnp.where(keep_prev[None], jnp.exp(sp - m), 0.0).astype(BF16)
    outs = []
    for p in range(SWA_HEADS // 2):
        v_cur, v_prev = vbd[p // pairs_per_kv]
        a_cur = jnp.concatenate([p_cur[2 * p], p_cur[2 * p + 1]], axis=1)
        a_prev = jnp.concatenate([p_prev[2 * p], p_prev[2 * p + 1]], axis=1)
        o = (jnp.dot(a_cur, v_cur, preferred_element_type=F32)
             + jnp.dot(a_prev, v_prev, preferred_element_type=F32))
        outs.append(o[:, :LANES] / o[:, LANES:])
    return jnp.concatenate(outs, axis=1)


def _mixer_kernel(sinks_ref, sbq_ref, sbkt_ref, sbv_lo_ref, sbv_hi_ref,
                  swq_ref, swkt_ref, swvbd_ref, tri_ref, ones_ref, gsb_ref, gsw_ref,
                  o_ref, acc_ref, run_ref):
    pending = [
        _mixer_block(blk, pl.program_id(1) * MIXER_BLOCKS + blk, sinks_ref, sbq_ref, sbkt_ref,
                     sbv_lo_ref, sbv_hi_ref, swq_ref, swkt_ref, swvbd_ref, tri_ref, ones_ref,
                     gsb_ref, gsw_ref, o_ref)
        for blk in range(MIXER_BLOCKS)]
    for blk, sweep_rest in enumerate(pending):
        sweep_rest(acc_ref.at[blk], run_ref.at[blk])


def _mixer_block(blk, i, sinks_ref, sbq_ref, sbkt_ref, sbv_lo_ref, sbv_hi_ref,
                 swq_ref, swkt_ref, swvbd_ref, tri_ref, ones_ref, gsb_ref, gsw_ref, o_ref):
    qrows = slice(blk * BLOCK, (blk + 1) * BLOCK)
    tri = tri_ref[...]
    q = [sbq_ref[0, qrows, p * LANES:(p + 1) * LANES] for p in range(SB_PAIRS)]
    scores = functools.partial(_sb_scores, q, sbkt_ref, tri)
    values = functools.partial(_sb_values, sbv_lo_ref, sbv_hi_ref)

    sw_o = _swa(i, qrows, sinks_ref, swq_ref, swkt_ref, swvbd_ref, ones_ref[...])
    o_ref[0, qrows, SB_W:] = _rms(sw_o, gsw_ref[...]).astype(BF16)

    diag = (lax.broadcasted_iota(jnp.int32, (BLOCK, BLOCK), 1)
            < lax.broadcasted_iota(jnp.int32, (BLOCK, BLOCK), 0))[None]
    k0s = [pl.multiple_of(jnp.maximum(i - t, 0) * BLOCK, BLOCK)
           for t in range(SB_UNROLLED_TILES)]
    front, run = [], None
    for t in range(SB_UNROLLED_TILES):
        z, incl, run = scores(k0s[t], strict=diag if t == 0 else None, run=run)
        front.append((z, incl))
    more = (i >= SB_UNROLLED_TILES) & (jnp.max(run) > EXP_ZERO_BELOW)
    acc = None
    for t, (z, incl) in enumerate(front):
        out = values(k0s[t], z, incl, mask=diag if t == 0 else None,
                     valid=None if t == 0 else i >= t)
        acc = out if acc is None else [a + o for a, o in zip(acc, out)]
    acc = jnp.concatenate(acc, axis=1)
    o_ref[0, qrows, :SB_W] = _rms(acc, gsb_ref[...]).astype(BF16)

    def sweep_rest(acc_ref, run_ref):
        def cond(carry):
            return carry[1]

        def body(carry):
            j, _ = carry
            k0 = pl.multiple_of(j * BLOCK, BLOCK)
            z, incl, new_run = scores(k0, run=run_ref[...])
            acc_ref[...] += jnp.concatenate(values(k0, z, incl), axis=1)
            run_ref[...] = new_run
            return j - 1, (j > 0) & (jnp.max(new_run) > EXP_ZERO_BELOW)

        @pl.when(more)
        def _():
            acc_ref[...] = acc
            run_ref[...] = run
            lax.while_loop(cond, body, (i - SB_UNROLLED_TILES, more))
            o_ref[0, qrows, :SB_W] = _rms(acc_ref[...], gsb_ref[...]).astype(BF16)

    return sweep_rest


def _mixer(sinks, sbq, sbkt, sbv_lo, sbv_hi, swq, swkt, swvbd, tri, ones_bd, gsb, gsw):
    b, s, _ = sbq.shape
    const = lambda bi, qi: (0, 0)
    resident = lambda a: pl.BlockSpec(a.shape, const, pipeline_mode=pl.Buffered(1))
    step_rows = MIXER_BLOCKS * BLOCK
    qblk = lambda w: pl.BlockSpec((1, step_rows, w), lambda bi, qi: (bi, qi, 0))
    per_batch = lambda r, c: pl.BlockSpec((1, r, c), lambda bi, qi: (bi, 0, 0))
    return pl.pallas_call(
        _mixer_kernel,
        grid=(b, s // step_rows),
        in_specs=[
            pl.BlockSpec(memory_space=pltpu.SMEM),
            qblk(SB_W),
            per_batch(SB_W, s),
            per_batch(s, SB_W),
            per_batch(s, SB_W),
            qblk(SWA_Q_W),
            per_batch(SWA_KV_W, s),
            per_batch(s, SWA_VBD_W),
            resident(tri),
            resident(ones_bd),
            resident(gsb),
            resident(gsw),
        ],
        out_specs=qblk(MIX_W),
        out_shape=jax.ShapeDtypeStruct((b, s, MIX_W), BF16),
        scratch_shapes=[pltpu.VMEM((MIXER_BLOCKS, BLOCK, SB_W), F32),
                        pltpu.VMEM((MIXER_BLOCKS, SB_HEADS, BLOCK, BLOCK), F32)],
        compiler_params=pltpu.CompilerParams(
            dimension_semantics=("arbitrary", "arbitrary"),
            vmem_limit_bytes=VMEM_LIMIT_BYTES),
        name="mixer",
    )(sinks, sbq, sbkt, sbv_lo, sbv_hi, swq, swkt, swvbd, tri, ones_bd, gsb, gsw)


def _mlp_kernel(x_ref, mix_ref, wout_ref, g_ref, wup_ref, wdown_ref, gfin_ref, o_ref, *,
                final_norm):
    x = x_ref[...] + jnp.dot(mix_ref[...], wout_ref[...], preferred_element_type=F32)
    h = _rms(x, g_ref[...]).astype(BF16)
    acc = x
    for c in range(D_FF // FF_CHUNK):
        u = jnp.dot(h, wup_ref[:, c * FF_CHUNK:(c + 1) * FF_CHUNK],
                    preferred_element_type=F32)
        u = jnp.square(jnp.maximum(u, 0.0)).astype(BF16)
        acc = acc + jnp.dot(u, wdown_ref[c * FF_CHUNK:(c + 1) * FF_CHUNK, :],
                            preferred_element_type=F32)
    if final_norm:
        acc = _rms(acc, gfin_ref[...])
    o_ref[...] = acc


def _mlp(x2d, mix2d, wout, g, wup, wdown, gfin, final_norm):
    n, d = x2d.shape
    t = MLP_TILE
    const = lambda ti: (0, 0)
    resident = lambda a: pl.BlockSpec(a.shape, const, pipeline_mode=pl.Buffered(1))
    return pl.pallas_call(
        functools.partial(_mlp_kernel, final_norm=final_norm),
        grid=(n // t,),
        in_specs=[
            pl.BlockSpec((t, d), lambda ti: (ti, 0)),
            pl.BlockSpec((t, MIX_W), lambda ti: (ti, 0)),
            resident(wout),
            resident(g),
            resident(wup),
            resident(wdown),
            resident(gfin),
        ],
        out_specs=pl.BlockSpec((t, d), lambda ti: (ti, 0)),
        out_shape=jax.ShapeDtypeStruct((n, d), F32),
        compiler_params=pltpu.CompilerParams(
            dimension_semantics=("arbitrary",),
            vmem_limit_bytes=VMEM_LIMIT_BYTES),
        name="mlp",
    )(x2d, mix2d, wout, g, wup, wdown, gfin)


def _in_proj_offsets():
    widths = [SB_W, SB_W, SB_W, SWA_Q_W, SWA_KV_W, SWA_KV_W]
    offs = [0]
    for w in widths:
        offs.append(offs[-1] + w)
    return offs


def kernel(x, positions, attn_norm, w_in, sb_norm, swa_norm, sinks, w_out, mlp_norm, w_up,
           w_down, final_norm):
    b, s, d = x.shape
    depth = w_in.shape[0]
    scale = 1.0 / math.sqrt(HEAD_DIM)
    tables = _rope_tables(positions)

    r = lax.broadcasted_iota(jnp.int32, (BLOCK, BLOCK), 0)
    c = lax.broadcasted_iota(jnp.int32, (BLOCK, BLOCK), 1)
    tri = jnp.concatenate([jnp.where(r >= c, -1.0, 0.0), jnp.full((BLOCK, BLOCK), -1.0)],
                          axis=1).astype(BF16)
    tri = jnp.concatenate([tri, tri], axis=0)
    rr = lax.broadcasted_iota(jnp.int32, (2 * BLOCK, LANES), 0)
    cc = lax.broadcasted_iota(jnp.int32, (2 * BLOCK, LANES), 1)
    ones_bd = (rr // BLOCK == cc // HEAD_DIM).astype(BF16)

    o = _in_proj_offsets()
    for l in range(depth):
        w = w_in[l]
        wq = (w[:, o[0]:o[1]] * scale).astype(BF16)
        wkt = w[:, o[1]:o[2]].T.astype(BF16)
        wv = w[:, o[2]:o[3]].astype(BF16)
        wsq = (w[:, o[3]:o[4]] * scale).astype(BF16)
        wskt = w[:, o[4]:o[5]].T.astype(BF16)
        wsv = w[:, o[5]:o[6]].astype(BF16)
        sbq, sbkt, sbv_lo, sbv_hi, swq, swkt, swvbd = _in_proj(
            x, attn_norm[l].reshape(1, d), tables, wq, wkt, wv, wsq, wskt, wsv)
        mix = _mixer(sinks[l], sbq, sbkt, sbv_lo, sbv_hi, swq, swkt, swvbd, tri, ones_bd,
                     sb_norm[l].reshape(1, SB_W), swa_norm[l].reshape(1, SWA_Q_W))
        x = _mlp(x.reshape(b * s, d), mix.reshape(b * s, MIX_W), w_out[l].astype(BF16),
                 mlp_norm[l].reshape(1, d), w_up[l].astype(BF16), w_down[l].astype(BF16),
                 final_norm.reshape(1, d), final_norm=(l == depth - 1)).reshape(b, s, d)
    return x
```

```python
import functools
import math

import jax
import jax.numpy as jnp
from jax import lax
from jax.experimental import pallas as pl
from jax.experimental.pallas import tpu as pltpu

D_MODEL = 1024
HEAD_DIM = 64
HALF = HEAD_DIM // 2
SB_HEADS = 8
SWA_HEADS = 8
SWA_KV_HEADS = 2
SWA_GROUP = SWA_HEADS // SWA_KV_HEADS
BLOCK = 128
D_FF = 4 * D_MODEL
ROPE_THETA = 10000.0
EPS = 1e-6
NEG = -1e30
EXP_ZERO_BELOW = -105.0
LOG2_E = 1.4426950408889634

SB_PAIRS = SB_HEADS // 2
SB_UNROLLED_TILES = 3
MIXER_BLOCKS = 4
SWA_BATCH = 2
SB_W = SB_HEADS * HEAD_DIM
SWA_Q_W = SWA_HEADS * HEAD_DIM
SWA_KV_W = SWA_KV_HEADS * HEAD_DIM
SWA_VBD_W = 2 * SWA_KV_HEADS * 2 * HEAD_DIM
MIX_W = SB_W + SWA_Q_W

LANES = 128
VMEM_LIMIT_BYTES = 56 * 1024 * 1024

PROJ_TILE = 1024
MLP_TILE = 1024
FF_CHUNK = 1024

BF16 = jnp.bfloat16
F32 = jnp.float32
NT_DIMS = (((1,), (1,)), ((), ()))


def _rms(x, g):
    return x * lax.rsqrt(jnp.mean(x * x, axis=-1, keepdims=True) + EPS) * g


def _rope_table_kernel(pos_col_ref, pos_row_ref, freq_row_ref, freq_col_ref,
                       cos_ref, sin_ref, cos_t_ref, sin_t_ref):
    ang = pos_col_ref[...].astype(F32) * freq_row_ref[...]
    lane = lax.broadcasted_iota(jnp.int32, ang.shape, 1)
    first_half = (lane % HEAD_DIM) < HALF
    cos_ref[...] = jnp.cos(ang)
    s = jnp.sin(ang)
    sin_ref[...] = jnp.where(first_half, -s, s)
    ang_t = freq_col_ref[...] * pos_row_ref[...].astype(F32)
    cos_t_ref[...] = jnp.cos(ang_t)
    sin_t_ref[...] = jnp.sin(ang_t)


def _rope_tables(positions):
    s = positions.shape[0]
    inv_freq = 1.0 / (ROPE_THETA ** (jnp.arange(HALF, dtype=F32) * (2.0 / HEAD_DIM)))
    freq_row = jnp.tile(inv_freq, LANES // HALF).reshape(1, LANES)
    return pl.pallas_call(
        _rope_table_kernel,
        out_shape=(jax.ShapeDtypeStruct((s, LANES), F32),
                   jax.ShapeDtypeStruct((s, LANES), F32),
                   jax.ShapeDtypeStruct((HALF, s), F32),
                   jax.ShapeDtypeStruct((HALF, s), F32)),
        name="rope_tables",
    )(positions.reshape(s, 1), positions.reshape(1, s), freq_row, inv_freq.reshape(HALF, 1))


def _rope(x, cos, sin_signed):
    lane = lax.broadcasted_iota(jnp.int32, (x.shape[0], LANES), 1)
    first_half = (lane % HEAD_DIM) < HALF
    outs = []
    for c in range(x.shape[1] // LANES):
        xc = x[:, c * LANES:(c + 1) * LANES]
        swapped = jnp.where(first_half,
                            pltpu.roll(xc, LANES - HALF, axis=1),
                            pltpu.roll(xc, HALF, axis=1))
        outs.append(xc * cos + swapped * sin_signed)
    return jnp.concatenate(outs, axis=1) if len(outs) > 1 else outs[0]


def _rope_t(xt, cos_t, sin_t):
    outs = []
    for h in range(xt.shape[0] // HEAD_DIM):
        x1 = xt[h * HEAD_DIM:h * HEAD_DIM + HALF]
        x2 = xt[h * HEAD_DIM + HALF:(h + 1) * HEAD_DIM]
        outs += [x1 * cos_t - x2 * sin_t, x2 * cos_t + x1 * sin_t]
    return jnp.concatenate(outs, axis=0)


def _in_proj_kernel(x_ref, g_ref, cos_ref, sin_ref, cos_t_ref, sin_t_ref,
                    wq_ref, wkt_ref, wv_ref, wsq_ref, wskt_ref, wsv_ref,
                    sbq_ref, sbkt_ref, sbv_lo_ref, sbv_hi_ref, swq_ref, swkt_ref, swvbd_ref):
    h = _rms(x_ref[0], g_ref[...]).astype(BF16)
    t = h.shape[0]
    dot = functools.partial(jnp.dot, preferred_element_type=F32)
    dot_nt = functools.partial(lax.dot_general, dimension_numbers=NT_DIMS,
                               preferred_element_type=F32)
    sbq_ref[0] = dot(h, wq_ref[...]).astype(BF16)
    sbkt_ref[0] = dot_nt(wkt_ref[...], h).astype(BF16)
    v = dot(h, wv_ref[...])
    even_head = (lax.broadcasted_iota(jnp.int32, (t, SB_W), 1) // HEAD_DIM) % 2 == 0
    sbv_lo_ref[0] = jnp.where(even_head, v, 0.0).astype(BF16)
    sbv_hi_ref[0] = jnp.where(even_head, 0.0, v).astype(BF16)
    swq_ref[0] = _rope(dot(h, wsq_ref[...]), cos_ref[...], sin_ref[...]).astype(BF16)
    swkt_ref[0] = _rope_t(dot_nt(wskt_ref[...], h), cos_t_ref[...], sin_t_ref[...]).astype(BF16)
    sv = dot(h, wsv_ref[...])
    sv_swapped = pltpu.roll(sv, HEAD_DIM, axis=1)
    lo = lax.broadcasted_iota(jnp.int32, (t, LANES), 1) < HEAD_DIM
    swvbd_ref[0] = jnp.concatenate(
        [jnp.where(lo, sv, 0.0), jnp.where(lo, 0.0, sv_swapped),
         jnp.where(lo, sv_swapped, 0.0), jnp.where(lo, 0.0, sv)], axis=1).astype(BF16)


def _in_proj(x, g, tables, wq, wkt, wv, wsq, wskt, wsv):
    b, s, d = x.shape
    t = PROJ_TILE
    cos, sin, cos_t, sin_t = tables
    const = lambda bi, ti: (0, 0)
    tok = lambda w: pl.BlockSpec((1, t, w), lambda bi, ti: (bi, ti, 0))
    feat = lambda w: pl.BlockSpec((1, w, t), lambda bi, ti: (bi, 0, ti))
    out_shape = (
        jax.ShapeDtypeStruct((b, s, SB_W), BF16),
        jax.ShapeDtypeStruct((b, SB_W, s), BF16),
        jax.ShapeDtypeStruct((b, s, SB_W), BF16),
        jax.ShapeDtypeStruct((b, s, SB_W), BF16),
        jax.ShapeDtypeStruct((b, s, SWA_Q_W), BF16),
        jax.ShapeDtypeStruct((b, SWA_KV_W, s), BF16),
        jax.ShapeDtypeStruct((b, s, SWA_VBD_W), BF16),
    )
    return pl.pallas_call(
        _in_proj_kernel,
        grid=(b, s // t),
        in_specs=[
            tok(d),
            pl.BlockSpec((1, d), const),
            pl.BlockSpec((t, LANES), lambda bi, ti: (ti, 0)),
            pl.BlockSpec((t, LANES), lambda bi, ti: (ti, 0)),
            pl.BlockSpec((HALF, t), lambda bi, ti: (0, ti)),
            pl.BlockSpec((HALF, t), lambda bi, ti: (0, ti)),
            pl.BlockSpec(wq.shape, const),
            pl.BlockSpec(wkt.shape, const),
            pl.BlockSpec(wv.shape, const),
            pl.BlockSpec(wsq.shape, const),
            pl.BlockSpec(wskt.shape, const),
            pl.BlockSpec(wsv.shape, const),
        ],
        out_specs=(tok(SB_W), feat(SB_W), tok(SB_W), tok(SB_W),
                   tok(SWA_Q_W), feat(SWA_KV_W), tok(SWA_VBD_W)),
        out_shape=out_shape,
        compiler_params=pltpu.CompilerParams(
            dimension_semantics=("arbitrary", "arbitrary"),
            vmem_limit_bytes=VMEM_LIMIT_BYTES),
        name="in_proj",
    )(x, g, cos, sin, cos_t, sin_t, wq, wkt, wv, wsq, wskt, wsv)


def _block_diag_keys(kt0, kt1):
    z = jnp.zeros_like(kt0)
    return jnp.concatenate([jnp.concatenate([kt0, z], axis=1),
                            jnp.concatenate([z, kt1], axis=1)], axis=0)


def _sb_scores(q, kt_ref, tri, k0, strict=None, run=None):
    rows = BLOCK
    zs = []
    for p in range(SB_PAIRS):
        kt = kt_ref[0, p * LANES:(p + 1) * LANES, pl.ds(k0, BLOCK)]
        zp = jnp.dot(q[p], _block_diag_keys(kt[:HEAD_DIM], kt[HEAD_DIM:]),
                     preferred_element_type=F32)
        zs += [zp[:, :BLOCK], zp[:, BLOCK:]]
    z = jnp.stack(zs)
    sp = jnp.maximum(z, 0.0) + jnp.log(1.0 + jnp.exp2(jnp.abs(z) * (-LOG2_E)))
    if strict is not None:
        sp = jnp.where(strict, sp, 0.0)
    sp = sp.reshape(SB_HEADS * rows, BLOCK)
    hi = sp.astype(BF16)
    lo = (sp - hi.astype(F32)).astype(BF16)
    cs = jnp.dot(jnp.concatenate([hi, lo], axis=1), tri, preferred_element_type=F32)
    if run is not None:
        run2d = run.reshape(SB_HEADS * rows, BLOCK)
        cs = cs + jnp.concatenate([run2d, run2d], axis=1)
    incl = cs[:, :BLOCK].reshape(SB_HEADS, rows, BLOCK)
    total = cs[:, BLOCK:].reshape(SB_HEADS, rows, BLOCK)
    return z, incl, total


def _sb_values(vlo_ref, vhi_ref, k0, z, incl, mask=None, valid=None):
    a = jnp.exp(z + incl)
    if mask is not None:
        a = jnp.where(mask, a, 0.0)
    a = a.astype(BF16)
    out = []
    for p in range(SB_PAIRS):
        cols = slice(p * LANES, (p + 1) * LANES)
        vbd = jnp.concatenate([vlo_ref[0, pl.ds(k0, BLOCK), cols],
                               vhi_ref[0, pl.ds(k0, BLOCK), cols]], axis=0)
        if valid is not None:
            vbd = jnp.where(valid, vbd, jnp.zeros_like(vbd))
        a_pair = jnp.concatenate([a[2 * p], a[2 * p + 1]], axis=1)
        out.append(jnp.dot(a_pair, vbd, preferred_element_type=F32))
    return out


def _swa(i0, blk0, nblk, sinks_ref, q_ref, kt_ref, vbd_ref, ones_bd):
    blocks = range(nblk)
    pairs_per_kv = SWA_GROUP // 2
    k0 = [pl.multiple_of(jnp.maximum(i0 - 1 + t, 0) * BLOCK, BLOCK) for t in range(nblk + 1)]
    qpos = lax.broadcasted_iota(jnp.int32, (BLOCK, BLOCK), 0)
    kpos = lax.broadcasted_iota(jnp.int32, (BLOCK, BLOCK), 1)
    m_cur = (kpos <= qpos)[None]
    sink_slot = kpos == 0
    m_prev = [((kpos > qpos) & (i0 > 0) if b == 0 else kpos > qpos)[None] for b in blocks]
    keep_prev = [m | sink_slot[None] for m in m_prev]
    first_key = lax.broadcasted_iota(jnp.int32, (2 * BLOCK, LANES), 0) % BLOCK == 0
    s_cur, s_prev = [[] for _ in blocks], [[] for _ in blocks]
    v_cur, v_prev = [], []
    for g in range(SWA_KV_HEADS):
        rows = slice(g * HEAD_DIM, (g + 1) * HEAD_DIM)
        kbd = []
        for t in range(nblk + 1):
            kt = kt_ref[0, rows, pl.ds(k0[t], BLOCK)]
            kbd.append(_block_diag_keys(kt, kt))
        for b in blocks:
            for pp in range(pairs_per_kv):
                p = g * pairs_per_kv + pp
                q_pair = q_ref[0, (blk0 + b) * BLOCK:(blk0 + b + 1) * BLOCK,
                               p * LANES:(p + 1) * LANES]
                sc = jnp.dot(q_pair, kbd[b + 1], preferred_element_type=F32)
                sp = jnp.dot(q_pair, kbd[b], preferred_element_type=F32)
                s_cur[b] += [sc[:, :BLOCK], sc[:, BLOCK:]]
                s_prev[b] += [sp[:, :BLOCK], sp[:, BLOCK:]]
        cols = slice(2 * g * LANES, 2 * (g + 1) * LANES)
        vc, vp = [], []
        for t in range(nblk + 1):
            v = vbd_ref[0, pl.ds(k0[t], BLOCK), cols]
            v = jnp.concatenate([v[:, :LANES], v[:, LANES:]], axis=0)
            vc.append(jnp.concatenate([v, ones_bd], axis=1))
            v = jnp.where(first_key, jnp.zeros_like(v), v)
            vp.append(jnp.concatenate([v, ones_bd], axis=1))
        v_cur.append(vc)
        v_prev.append(vp)
    fill = jnp.stack([jnp.where(sink_slot, sinks_ref[h], NEG) for h in range(SWA_HEADS)])
    sc = [jnp.where(m_cur, jnp.stack(s_cur[b]), NEG) for b in blocks]
    sp = [jnp.where(m_prev[b], jnp.stack(s_prev[b]), fill) for b in blocks]
    m = [jnp.max(jnp.maximum(sc[b], sp[b]), axis=2, keepdims=True) for b in blocks]
    p_cur = [jnp.where(m_cur, jnp.exp(sc[b] - m[b]), 0.0).astype(BF16) for b in blocks]
    p_prev = [jnp.where(keep_prev[b], jnp.exp(sp[b] - m[b]), 0.0).astype(BF16) for b in blocks]
    outs = []
    for b in blocks:
        out = []
        for p in range(SWA_HEADS // 2):
            g = p // pairs_per_kv
            a_cur = jnp.concatenate([p_cur[b][2 * p], p_cur[b][2 * p + 1]], axis=1)
            a_prev = jnp.concatenate([p_prev[b][2 * p], p_prev[b][2 * p + 1]], axis=1)
            o = (jnp.dot(a_cur, v_cur[g][b + 1], preferred_element_type=F32)
                 + jnp.dot(a_prev, v_prev[g][b], preferred_element_type=F32))
            out.append(o[:, :LANES] / o[:, LANES:])
        outs.append(jnp.concatenate(out, axis=1))
    return outs


def _mixer_kernel(sinks_ref, sbq_ref, sbkt_ref, sbv_lo_ref, sbv_hi_ref,
                  swq_ref, swkt_ref, swvbd_ref, tri_ref, ones_ref, gsb_ref, gsw_ref,
                  o_ref, acc_ref, run_ref):
    i0 = pl.program_id(1) * MIXER_BLOCKS
    pending = []
    for blk0 in range(0, MIXER_BLOCKS, SWA_BATCH):
        sw_o = _swa(i0 + blk0, blk0, SWA_BATCH, sinks_ref, swq_ref, swkt_ref, swvbd_ref,
                    ones_ref[...])
        for b in range(SWA_BATCH):
            o_ref[0, (blk0 + b) * BLOCK:(blk0 + b + 1) * BLOCK, SB_W:] = (
                _rms(sw_o[b], gsw_ref[...]).astype(BF16))
        pending += [
            _sb_block(blk, i0 + blk, sbq_ref, sbkt_ref, sbv_lo_ref, sbv_hi_ref, tri_ref,
                      gsb_ref, o_ref)
            for blk in range(blk0, blk0 + SWA_BATCH)]
    for blk, sweep_rest in enumerate(pending):
        sweep_rest(acc_ref.at[blk], run_ref.at[blk])


def _sb_block(blk, i, sbq_ref, sbkt_ref, sbv_lo_ref, sbv_hi_ref, tri_ref, gsb_ref, o_ref):
    qrows = slice(blk * BLOCK, (blk + 1) * BLOCK)
    tri = tri_ref[...]
    q = [sbq_ref[0, qrows, p * LANES:(p + 1) * LANES] for p in range(SB_PAIRS)]
    scores = functools.partial(_sb_scores, q, sbkt_ref, tri)
    values = functools.partial(_sb_values, sbv_lo_ref, sbv_hi_ref)

    diag = (lax.broadcasted_iota(jnp.int32, (BLOCK, BLOCK), 1)
            < lax.broadcasted_iota(jnp.int32, (BLOCK, BLOCK), 0))[None]
    k0s = [pl.multiple_of(jnp.maximum(i - t, 0) * BLOCK, BLOCK)
           for t in range(SB_UNROLLED_TILES)]
    front, run = [], None
    for t in range(SB_UNROLLED_TILES):
        z, incl, run = scores(k0s[t], strict=diag if t == 0 else None, run=run)
        front.append((z, incl))
    more = (i >= SB_UNROLLED_TILES) & (jnp.max(run) > EXP_ZERO_BELOW)
    acc = None
    for t, (z, incl) in enumerate(front):
        out = values(k0s[t], z, incl, mask=diag if t == 0 else None,
                     valid=None if t == 0 else i >= t)
        acc = out if acc is None else [a + o for a, o in zip(acc, out)]
    acc = jnp.concatenate(acc, axis=1)
    o_ref[0, qrows, :SB_W] = _rms(acc, gsb_ref[...]).astype(BF16)

    def sweep_rest(acc_ref, run_ref):
        def cond(carry):
            return carry[1]

        def body(carry):
            j, _ = carry
            k0 = pl.multiple_of(j * BLOCK, BLOCK)
            z, incl, new_run = scores(k0, run=run_ref[...])
            acc_ref[...] += jnp.concatenate(values(k0, z, incl), axis=1)
            run_ref[...] = new_run
            return j - 1, (j > 0) & (jnp.max(new_run) > EXP_ZERO_BELOW)

        @pl.when(more)
        def _():
            acc_ref[...] = acc
            run_ref[...] = run
            lax.while_loop(cond, body, (i - SB_UNROLLED_TILES, more))
            o_ref[0, qrows, :SB_W] = _rms(acc_ref[...], gsb_ref[...]).astype(BF16)

    return sweep_rest


def _mixer(sinks, sbq, sbkt, sbv_lo, sbv_hi, swq, swkt, swvbd, tri, ones_bd, gsb, gsw):
    b, s, _ = sbq.shape
    const = lambda bi, qi: (0, 0)
    resident = lambda a: pl.BlockSpec(a.shape, const, pipeline_mode=pl.Buffered(1))
    step_rows = MIXER_BLOCKS * BLOCK
    qblk = lambda w: pl.BlockSpec((1, step_rows, w), lambda bi, qi: (bi, qi, 0))
    per_batch = lambda r, c: pl.BlockSpec((1, r, c), lambda bi, qi: (bi, 0, 0))
    return pl.pallas_call(
        _mixer_kernel,
        grid=(b, s // step_rows),
        in_specs=[
            pl.BlockSpec(memory_space=pltpu.SMEM),
            qblk(SB_W),
            per_batch(SB_W, s),
            per_batch(s, SB_W),
            per_batch(s, SB_W),
            qblk(SWA_Q_W),
            per_batch(SWA_KV_W, s),
            per_batch(s, SWA_VBD_W),
            resident(tri),
            resident(ones_bd),
            resident(gsb),
            resident(gsw),
        ],
        out_specs=qblk(MIX_W),
        out_shape=jax.ShapeDtypeStruct((b, s, MIX_W), BF16),
        scratch_shapes=[pltpu.VMEM((MIXER_BLOCKS, BLOCK, SB_W), F32),
                        pltpu.VMEM((MIXER_BLOCKS, SB_HEADS, BLOCK, BLOCK), F32)],
        compiler_params=pltpu.CompilerParams(
            dimension_semantics=("arbitrary", "arbitrary"),
            vmem_limit_bytes=VMEM_LIMIT_BYTES),
        name="mixer",
    )(sinks, sbq, sbkt, sbv_lo, sbv_hi, swq, swkt, swvbd, tri, ones_bd, gsb, gsw)


def _mlp_kernel(x_ref, mix_ref, wout_ref, g_ref, wup_ref, wdown_ref, gfin_ref, o_ref, *,
                final_norm):
    x = x_ref[...] + jnp.dot(mix_ref[...], wout_ref[...], preferred_element_type=F32)
    h = _rms(x, g_ref[...]).astype(BF16)
    acc = x
    for c in range(D_FF // FF_CHUNK):
        u = jnp.dot(h, wup_ref[:, c * FF_CHUNK:(c + 1) * FF_CHUNK],
                    preferred_element_type=F32)
        u = jnp.square(jnp.maximum(u, 0.0)).astype(BF16)
        acc = acc + jnp.dot(u, wdown_ref[c * FF_CHUNK:(c + 1) * FF_CHUNK, :],
                            preferred_element_type=F32)
    if final_norm:
        acc = _rms(acc, gfin_ref[...])
    o_ref[...] = acc


def _mlp(x2d, mix2d, wout, g, wup, wdown, gfin, final_norm):
    n, d = x2d.shape
    t = MLP_TILE
    const = lambda ti: (0, 0)
    resident = lambda a: pl.BlockSpec(a.shape, const, pipeline_mode=pl.Buffered(1))
    return pl.pallas_call(
        functools.partial(_mlp_kernel, final_norm=final_norm),
        grid=(n // t,),
        in_specs=[
            pl.BlockSpec((t, d), lambda ti: (ti, 0)),
            pl.BlockSpec((t, MIX_W), lambda ti: (ti, 0)),
            resident(wout),
            resident(g),
            resident(wup),
            resident(wdown),
            resident(gfin),
        ],
        out_specs=pl.BlockSpec((t, d), lambda ti: (ti, 0)),
        out_shape=jax.ShapeDtypeStruct((n, d), F32),
        compiler_params=pltpu.CompilerParams(
            dimension_semantics=("arbitrary",),
            vmem_limit_bytes=VMEM_LIMIT_BYTES),
        name="mlp",
    )(x2d, mix2d, wout, g, wup, wdown, gfin)


def _in_proj_offsets():
    widths = [SB_W, SB_W, SB_W, SWA_Q_W, SWA_KV_W, SWA_KV_W]
    offs = [0]
    for w in widths:
        offs.append(offs[-1] + w)
    return offs


def kernel(x, positions, attn_norm, w_in, sb_norm, swa_norm, sinks, w_out, mlp_norm, w_up,
           w_down, final_norm):
    b, s, d = x.shape
    depth = w_in.shape[0]
    scale = 1.0 / math.sqrt(HEAD_DIM)
    tables = _rope_tables(positions)

    r = lax.broadcasted_iota(jnp.int32, (BLOCK, BLOCK), 0)
    c = lax.broadcasted_iota(jnp.int32, (BLOCK, BLOCK), 1)
    tri = jnp.concatenate([jnp.where(r >= c, -1.0, 0.0), jnp.full((BLOCK, BLOCK), -1.0)],
                          axis=1).astype(BF16)
    tri = jnp.concatenate([tri, tri], axis=0)
    rr = lax.broadcasted_iota(jnp.int32, (2 * BLOCK, LANES), 0)
    cc = lax.broadcasted_iota(jnp.int32, (2 * BLOCK, LANES), 1)
    ones_bd = (rr // BLOCK == cc // HEAD_DIM).astype(BF16)

    o = _in_proj_offsets()
    for l in range(depth):
        w = w_in[l]
        wq = (w[:, o[0]:o[1]] * scale).astype(BF16)
        wkt = w[:, o[1]:o[2]].T.astype(BF16)
        wv = w[:, o[2]:o[3]].astype(BF16)
        wsq = (w[:, o[3]:o[4]] * scale).astype(BF16)
        wskt = w[:, o[4]:o[5]].T.astype(BF16)
        wsv = w[:, o[5]:o[6]].astype(BF16)
        sbq, sbkt, sbv_lo, sbv_hi, swq, swkt, swvbd = _in_proj(
            x, attn_norm[l].reshape(1, d), tables, wq, wkt, wv, wsq, wskt, wsv)
        mix = _mixer(sinks[l], sbq, sbkt, sbv_lo, sbv_hi, swq, swkt, swvbd, tri, ones_bd,
                     sb_norm[l].reshape(1, SB_W), swa_norm[l].reshape(1, SWA_Q_W))
        x = _mlp(x.reshape(b * s, d), mix.reshape(b * s, MIX_W), w_out[l].astype(BF16),
                 mlp_norm[l].reshape(1, d), w_up[l].astype(BF16), w_down[l].astype(BF16),
                 final_norm.reshape(1, d), final_norm=(l == depth - 1)).reshape(b, s, d)
    return x
```

```python
import functools
import math

import jax
import jax.numpy as jnp
from jax import lax
from jax.experimental import pallas as pl
from jax.experimental.pallas import tpu as pltpu

D_MODEL = 1024
HEAD_DIM = 64
HALF = HEAD_DIM // 2
SB_HEADS = 8
SWA_HEADS = 8
SWA_KV_HEADS = 2
SWA_GROUP = SWA_HEADS // SWA_KV_HEADS
BLOCK = 128
D_FF = 4 * D_MODEL
ROPE_THETA = 10000.0
EPS = 1e-6
NEG = -1e30
EXP_ZERO_BELOW = -105.0
LOG2_E = 1.4426950408889634

SB_PAIRS = SB_HEADS // 2
SB_UNROLLED_TILES = 3
MIXER_BLOCKS = 4
SWA_BATCH = 2
SB_W = SB_HEADS * HEAD_DIM
SWA_Q_W = SWA_HEADS * HEAD_DIM
SWA_KV_W = SWA_KV_HEADS * HEAD_DIM
SWA_VBD_W = 2 * SWA_KV_HEADS * 2 * HEAD_DIM
MIX_W = SB_W + SWA_Q_W

LANES = 128
VMEM_LIMIT_BYTES = 56 * 1024 * 1024

PROJ_TILE = 1024
PROJ_SPLIT = 2
MLP_TILE = 1024
FF_CHUNK = 1024

BF16 = jnp.bfloat16
F32 = jnp.float32
NT_DIMS = (((1,), (1,)), ((), ()))


def _rms(x, g):
    return x * lax.rsqrt(jnp.mean(x * x, axis=-1, keepdims=True) + EPS) * g


def _rope_table_kernel(pos_col_ref, pos_row_ref, freq_row_ref, freq_col_ref,
                       cos_ref, sin_ref, cos_t_ref, sin_t_ref):
    ang = pos_col_ref[...].astype(F32) * freq_row_ref[...]
    lane = lax.broadcasted_iota(jnp.int32, ang.shape, 1)
    first_half = (lane % HEAD_DIM) < HALF
    cos_ref[...] = jnp.cos(ang)
    s = jnp.sin(ang)
    sin_ref[...] = jnp.where(first_half, -s, s)
    ang_t = freq_col_ref[...] * pos_row_ref[...].astype(F32)
    cos_t_ref[...] = jnp.cos(ang_t)
    sin_t_ref[...] = jnp.sin(ang_t)


def _rope_tables(positions):
    s = positions.shape[0]
    inv_freq = 1.0 / (ROPE_THETA ** (jnp.arange(HALF, dtype=F32) * (2.0 / HEAD_DIM)))
    freq_row = jnp.tile(inv_freq, LANES // HALF).reshape(1, LANES)
    return pl.pallas_call(
        _rope_table_kernel,
        out_shape=(jax.ShapeDtypeStruct((s, LANES), F32),
                   jax.ShapeDtypeStruct((s, LANES), F32),
                   jax.ShapeDtypeStruct((HALF, s), F32),
                   jax.ShapeDtypeStruct((HALF, s), F32)),
        name="rope_tables",
    )(positions.reshape(s, 1), positions.reshape(1, s), freq_row, inv_freq.reshape(HALF, 1))


def _rope(x, cos, sin_signed):
    lane = lax.broadcasted_iota(jnp.int32, (x.shape[0], LANES), 1)
    first_half = (lane % HEAD_DIM) < HALF
    outs = []
    for c in range(x.shape[1] // LANES):
        xc = x[:, c * LANES:(c + 1) * LANES]
        swapped = jnp.where(first_half,
                            pltpu.roll(xc, LANES - HALF, axis=1),
                            pltpu.roll(xc, HALF, axis=1))
        outs.append(xc * cos + swapped * sin_signed)
    return jnp.concatenate(outs, axis=1) if len(outs) > 1 else outs[0]


def _rope_t(xt, cos_t, sin_t):
    outs = []
    for h in range(xt.shape[0] // HEAD_DIM):
        x1 = xt[h * HEAD_DIM:h * HEAD_DIM + HALF]
        x2 = xt[h * HEAD_DIM + HALF:(h + 1) * HEAD_DIM]
        outs += [x1 * cos_t - x2 * sin_t, x2 * cos_t + x1 * sin_t]
    return jnp.concatenate(outs, axis=0)


def _in_proj_kernel(x_ref, g_ref, cos_ref, sin_ref, cos_t_ref, sin_t_ref,
                    wq_ref, wkt_ref, wv_ref, wsq_ref, wskt_ref, wsv_ref,
                    sbq_ref, sbkt_ref, sbv_lo_ref, sbv_hi_ref, swq_ref, swkt_ref, swvbd_ref):
    dot = functools.partial(jnp.dot, preferred_element_type=F32)
    dot_nt = functools.partial(lax.dot_general, dimension_numbers=NT_DIMS,
                               preferred_element_type=F32)
    t = x_ref.shape[1] // PROJ_SPLIT
    even_head = (lax.broadcasted_iota(jnp.int32, (t, SB_W), 1) // HEAD_DIM) % 2 == 0
    lo = lax.broadcasted_iota(jnp.int32, (t, LANES), 1) < HEAD_DIM
    for part in range(PROJ_SPLIT):
        rows = slice(part * t, (part + 1) * t)
        h = _rms(x_ref[0, rows], g_ref[...]).astype(BF16)
        sbq_ref[0, rows] = dot(h, wq_ref[...]).astype(BF16)
        sbkt_ref[0, :, rows] = dot_nt(wkt_ref[...], h).astype(BF16)
        v = dot(h, wv_ref[...])
        sbv_lo_ref[0, rows] = jnp.where(even_head, v, 0.0).astype(BF16)
        sbv_hi_ref[0, rows] = jnp.where(even_head, 0.0, v).astype(BF16)
        swq_ref[0, rows] = _rope(dot(h, wsq_ref[...]), cos_ref[rows], sin_ref[rows]).astype(BF16)
        swkt_ref[0, :, rows] = _rope_t(dot_nt(wskt_ref[...], h), cos_t_ref[:, rows],
                                       sin_t_ref[:, rows]).astype(BF16)
        sv = dot(h, wsv_ref[...])
        sv_swapped = pltpu.roll(sv, HEAD_DIM, axis=1)
        swvbd_ref[0, rows] = jnp.concatenate(
            [jnp.where(lo, sv, 0.0), jnp.where(lo, 0.0, sv_swapped),
             jnp.where(lo, sv_swapped, 0.0), jnp.where(lo, 0.0, sv)], axis=1).astype(BF16)


def _in_proj(x, g, tables, wq, wkt, wv, wsq, wskt, wsv):
    b, s, d = x.shape
    t = PROJ_TILE
    cos, sin, cos_t, sin_t = tables
    const = lambda bi, ti: (0, 0)
    tok = lambda w: pl.BlockSpec((1, t, w), lambda bi, ti: (bi, ti, 0))
    feat = lambda w: pl.BlockSpec((1, w, t), lambda bi, ti: (bi, 0, ti))
    out_shape = (
        jax.ShapeDtypeStruct((b, s, SB_W), BF16),
        jax.ShapeDtypeStruct((b, SB_W, s), BF16),
        jax.ShapeDtypeStruct((b, s, SB_W), BF16),
        jax.ShapeDtypeStruct((b, s, SB_W), BF16),
        jax.ShapeDtypeStruct((b, s, SWA_Q_W), BF16),
        jax.ShapeDtypeStruct((b, SWA_KV_W, s), BF16),
        jax.ShapeDtypeStruct((b, s, SWA_VBD_W), BF16),
    )
    return pl.pallas_call(
        _in_proj_kernel,
        grid=(b, s // t),
        in_specs=[
            tok(d),
            pl.BlockSpec((1, d), const),
            pl.BlockSpec((t, LANES), lambda bi, ti: (ti, 0)),
            pl.BlockSpec((t, LANES), lambda bi, ti: (ti, 0)),
            pl.BlockSpec((HALF, t), lambda bi, ti: (0, ti)),
            pl.BlockSpec((HALF, t), lambda bi, ti: (0, ti)),
            pl.BlockSpec(wq.shape, const),
            pl.BlockSpec(wkt.shape, const),
            pl.BlockSpec(wv.shape, const),
            pl.BlockSpec(wsq.shape, const),
            pl.BlockSpec(wskt.shape, const),
            pl.BlockSpec(wsv.shape, const),
        ],
        out_specs=(tok(SB_W), feat(SB_W), tok(SB_W), tok(SB_W),
                   tok(SWA_Q_W), feat(SWA_KV_W), tok(SWA_VBD_W)),
        out_shape=out_shape,
        compiler_params=pltpu.CompilerParams(
            dimension_semantics=("arbitrary", "arbitrary"),
            vmem_limit_bytes=VMEM_LIMIT_BYTES),
        name="in_proj",
    )(x, g, cos, sin, cos_t, sin_t, wq, wkt, wv, wsq, wskt, wsv)


def _block_diag_keys(kt0, kt1):
    z = jnp.zeros_like(kt0)
    return jnp.concatenate([jnp.concatenate([kt0, z], axis=1),
                            jnp.concatenate([z, kt1], axis=1)], axis=0)


def _sb_scores(q, kt_ref, tri, k0, strict=None, run=None):
    zs = []
    for p in range(SB_PAIRS):
        kt = kt_ref[0, p * LANES:(p + 1) * LANES, pl.ds(k0, BLOCK)]
        zp = jnp.dot(q[p], _block_diag_keys(kt[:HEAD_DIM], kt[HEAD_DIM:]),
                     preferred_element_type=F32)
        zs += [zp[:, :BLOCK], zp[:, BLOCK:]]
    z = jnp.stack(zs)
    sp = jnp.maximum(z, 0.0) + jnp.log(1.0 + jnp.exp2(jnp.abs(z) * (-LOG2_E)))
    if strict is not None:
        sp = jnp.where(strict, sp, 0.0)
    sp = sp.reshape(SB_HEADS * BLOCK, BLOCK)
    hi = sp.astype(BF16)
    lo = (sp - hi.astype(F32)).astype(BF16)
    cs = jnp.dot(jnp.concatenate([hi, lo], axis=1), tri, preferred_element_type=F32)
    if run is not None:
        run2d = run.reshape(SB_HEADS * BLOCK, BLOCK)
        cs = cs + jnp.concatenate([run2d, run2d], axis=1)
    incl = cs[:, :BLOCK].reshape(SB_HEADS, BLOCK, BLOCK)
    total = cs[:, BLOCK:].reshape(SB_HEADS, BLOCK, BLOCK)
    return z, incl, total


def _sb_values(vlo_ref, vhi_ref, k0, z, incl, mask=None, valid=None):
    a = jnp.exp(z + incl)
    if mask is not None:
        a = jnp.where(mask, a, 0.0)
    a = a.astype(BF16)
    out = []
    for p in range(SB_PAIRS):
        cols = slice(p * LANES, (p + 1) * LANES)
        vbd = jnp.concatenate([vlo_ref[0, pl.ds(k0, BLOCK), cols],
                               vhi_ref[0, pl.ds(k0, BLOCK), cols]], axis=0)
        if valid is not None:
            vbd = jnp.where(valid, vbd, jnp.zeros_like(vbd))
        a_pair = jnp.concatenate([a[2 * p], a[2 * p + 1]], axis=1)
        out.append(jnp.dot(a_pair, vbd, preferred_element_type=F32))
    return out


def _swa(i0, blk0, nblk, sinks_ref, q_ref, kt_ref, vbd_ref, ones_bd):
    blocks = range(nblk)
    pairs_per_kv = SWA_GROUP // 2
    k0 = [pl.multiple_of(jnp.maximum(i0 - 1 + t, 0) * BLOCK, BLOCK) for t in range(nblk + 1)]
    qpos = lax.broadcasted_iota(jnp.int32, (BLOCK, BLOCK), 0)
    kpos = lax.broadcasted_iota(jnp.int32, (BLOCK, BLOCK), 1)
    m_cur = (kpos <= qpos)[None]
    sink_slot = kpos == 0
    m_prev = [((kpos > qpos) & (i0 > 0) if b == 0 else kpos > qpos)[None] for b in blocks]
    keep_prev = [m | sink_slot[None] for m in m_prev]
    first_key = lax.broadcasted_iota(jnp.int32, (2 * BLOCK, LANES), 0) % BLOCK == 0
    s_cur, s_prev = [[] for _ in blocks], [[] for _ in blocks]
    v_cur, v_prev = [], []
    for g in range(SWA_KV_HEADS):
        rows = slice(g * HEAD_DIM, (g + 1) * HEAD_DIM)
        kbd = []
        for t in range(nblk + 1):
            kt = kt_ref[0, rows, pl.ds(k0[t], BLOCK)]
            kbd.append(_block_diag_keys(kt, kt))
        for b in blocks:
            for pp in range(pairs_per_kv):
                p = g * pairs_per_kv + pp
                q_pair = q_ref[0, (blk0 + b) * BLOCK:(blk0 + b + 1) * BLOCK,
                               p * LANES:(p + 1) * LANES]
                sc = jnp.dot(q_pair, kbd[b + 1], preferred_element_type=F32)
                sp = jnp.dot(q_pair, kbd[b], preferred_element_type=F32)
                s_cur[b] += [sc[:, :BLOCK], sc[:, BLOCK:]]
                s_prev[b] += [sp[:, :BLOCK], sp[:, BLOCK:]]
        cols = slice(2 * g * LANES, 2 * (g + 1) * LANES)
        vc, vp = [], []
        for t in range(nblk + 1):
            v = vbd_ref[0, pl.ds(k0[t], BLOCK), cols]
            v = jnp.concatenate([v[:, :LANES], v[:, LANES:]], axis=0)
            vc.append(jnp.concatenate([v, ones_bd], axis=1))
            v = jnp.where(first_key, jnp.zeros_like(v), v)
            vp.append(jnp.concatenate([v, ones_bd], axis=1))
        v_cur.append(vc)
        v_prev.append(vp)
    fill = jnp.stack([jnp.where(sink_slot, sinks_ref[h], NEG) for h in range(SWA_HEADS)])
    sc = [jnp.where(m_cur, jnp.stack(s_cur[b]), NEG) for b in blocks]
    sp = [jnp.where(m_prev[b], jnp.stack(s_prev[b]), fill) for b in blocks]
    m = [jnp.max(jnp.maximum(sc[b], sp[b]), axis=2, keepdims=True) for b in blocks]
    p_cur = [jnp.where(m_cur, jnp.exp(sc[b] - m[b]), 0.0).astype(BF16) for b in blocks]
    p_prev = [jnp.where(keep_prev[b], jnp.exp(sp[b] - m[b]), 0.0).astype(BF16) for b in blocks]
    outs = []
    for b in blocks:
        out = []
        for p in range(SWA_HEADS // 2):
            g = p // pairs_per_kv
            a_cur = jnp.concatenate([p_cur[b][2 * p], p_cur[b][2 * p + 1]], axis=1)
            a_prev = jnp.concatenate([p_prev[b][2 * p], p_prev[b][2 * p + 1]], axis=1)
            o = (jnp.dot(a_cur, v_cur[g][b + 1], preferred_element_type=F32)
                 + jnp.dot(a_prev, v_prev[g][b], preferred_element_type=F32))
            out.append(o[:, :LANES] / o[:, LANES:])
        outs.append(jnp.concatenate(out, axis=1))
    return outs


def _mixer_kernel(sinks_ref, sbq_ref, sbkt_ref, sbv_lo_ref, sbv_hi_ref,
                  swq_ref, swkt_ref, swvbd_ref, tri_ref, ones_ref, gsb_ref, gsw_ref,
                  o_ref, acc_ref, run_ref):
    i0 = pl.program_id(1) * MIXER_BLOCKS
    pending = []
    for blk0 in range(0, MIXER_BLOCKS, SWA_BATCH):
        sw_o = _swa(i0 + blk0, blk0, SWA_BATCH, sinks_ref, swq_ref, swkt_ref, swvbd_ref,
                    ones_ref[...])
        for b in range(SWA_BATCH):
            o_ref[0, (blk0 + b) * BLOCK:(blk0 + b + 1) * BLOCK, SB_W:] = (
                _rms(sw_o[b], gsw_ref[...]).astype(BF16))
        pending += [
            _sb_block(blk, i0 + blk, sbq_ref, sbkt_ref, sbv_lo_ref, sbv_hi_ref, tri_ref,
                      gsb_ref, o_ref)
            for blk in range(blk0, blk0 + SWA_BATCH)]
    for blk, sweep_rest in enumerate(pending):
        sweep_rest(acc_ref.at[blk], run_ref.at[blk])


def _sb_block(blk, i, sbq_ref, sbkt_ref, sbv_lo_ref, sbv_hi_ref, tri_ref, gsb_ref, o_ref):
    qrows = slice(blk * BLOCK, (blk + 1) * BLOCK)
    tri = tri_ref[...]
    q = [sbq_ref[0, qrows, p * LANES:(p + 1) * LANES] for p in range(SB_PAIRS)]
    scores = functools.partial(_sb_scores, q, sbkt_ref, tri)
    values = functools.partial(_sb_values, sbv_lo_ref, sbv_hi_ref)

    diag = (lax.broadcasted_iota(jnp.int32, (BLOCK, BLOCK), 1)
            < lax.broadcasted_iota(jnp.int32, (BLOCK, BLOCK), 0))[None]
    k0s = [pl.multiple_of(jnp.maximum(i - t, 0) * BLOCK, BLOCK)
           for t in range(SB_UNROLLED_TILES)]
    front, run = [], None
    for t in range(SB_UNROLLED_TILES):
        z, incl, run = scores(k0s[t], strict=diag if t == 0 else None, run=run)
        front.append((z, incl))
    more = (i >= SB_UNROLLED_TILES) & (jnp.max(run) > EXP_ZERO_BELOW)
    acc = None
    for t, (z, incl) in enumerate(front):
        out = values(k0s[t], z, incl, mask=diag if t == 0 else None,
                     valid=None if blk >= t else i >= t)
        acc = out if acc is None else [a + o for a, o in zip(acc, out)]
    acc = jnp.concatenate(acc, axis=1)
    o_ref[0, qrows, :SB_W] = _rms(acc, gsb_ref[...]).astype(BF16)

    def sweep_rest(acc_ref, run_ref):
        def cond(carry):
            return carry[1]

        def body(carry):
            j, _ = carry
            k0 = pl.multiple_of(j * BLOCK, BLOCK)
            z, incl, new_run = scores(k0, run=run_ref[...])
            acc_ref[...] += jnp.concatenate(values(k0, z, incl), axis=1)
            run_ref[...] = new_run
            return j - 1, (j > 0) & (jnp.max(new_run) > EXP_ZERO_BELOW)

        @pl.when(more)
        def _():
            acc_ref[...] = acc
            run_ref[...] = run
            lax.while_loop(cond, body, (i - SB_UNROLLED_TILES, more))
            o_ref[0, qrows, :SB_W] = _rms(acc_ref[...], gsb_ref[...]).astype(BF16)

    return sweep_rest


def _mixer(sinks, sbq, sbkt, sbv_lo, sbv_hi, swq, swkt, swvbd, tri, ones_bd, gsb, gsw):
    b, s, _ = sbq.shape
    const = lambda bi, qi: (0, 0)
    resident = lambda a: pl.BlockSpec(a.shape, const, pipeline_mode=pl.Buffered(1))
    step_rows = MIXER_BLOCKS * BLOCK
    qblk = lambda w: pl.BlockSpec((1, step_rows, w), lambda bi, qi: (bi, qi, 0))
    per_batch = lambda r, c: pl.BlockSpec((1, r, c), lambda bi, qi: (bi, 0, 0))
    return pl.pallas_call(
        _mixer_kernel,
        grid=(b, s // step_rows),
        in_specs=[
            pl.BlockSpec(memory_space=pltpu.SMEM),
            qblk(SB_W),
            per_batch(SB_W, s),
            per_batch(s, SB_W),
            per_batch(s, SB_W),
            qblk(SWA_Q_W),
            per_batch(SWA_KV_W, s),
            per_batch(s, SWA_VBD_W),
            resident(tri),
            resident(ones_bd),
            resident(gsb),
            resident(gsw),
        ],
        out_specs=qblk(MIX_W),
        out_shape=jax.ShapeDtypeStruct((b, s, MIX_W), BF16),
        scratch_shapes=[pltpu.VMEM((MIXER_BLOCKS, BLOCK, SB_W), F32),
                        pltpu.VMEM((MIXER_BLOCKS, SB_HEADS, BLOCK, BLOCK), F32)],
        compiler_params=pltpu.CompilerParams(
            dimension_semantics=("arbitrary", "arbitrary"),
            vmem_limit_bytes=VMEM_LIMIT_BYTES),
        name="mixer",
    )(sinks, sbq, sbkt, sbv_lo, sbv_hi, swq, swkt, swvbd, tri, ones_bd, gsb, gsw)


def _mlp_kernel(x_ref, mix_ref, wout_ref, g_ref, wup_ref, wdown_ref, gfin_ref, o_ref, *,
                final_norm):
    x = x_ref[...] + jnp.dot(mix_ref[...], wout_ref[...], preferred_element_type=F32)
    h = _rms(x, g_ref[...]).astype(BF16)
    acc = x
    for c in range(D_FF // FF_CHUNK):
        u = jnp.dot(h, wup_ref[:, c * FF_CHUNK:(c + 1) * FF_CHUNK],
                    preferred_element_type=F32)
        u = jnp.square(jnp.maximum(u, 0.0)).astype(BF16)
        acc = acc + jnp.dot(u, wdown_ref[c * FF_CHUNK:(c + 1) * FF_CHUNK, :],
                            preferred_element_type=F32)
    if final_norm:
        acc = _rms(acc, gfin_ref[...])
    o_ref[...] = acc


def _mlp(x2d, mix2d, wout, g, wup, wdown, gfin, final_norm):
    n, d = x2d.shape
    t = MLP_TILE
    const = lambda ti: (0, 0)
    resident = lambda a: pl.BlockSpec(a.shape, const, pipeline_mode=pl.Buffered(1))
    return pl.pallas_call(
        functools.partial(_mlp_kernel, final_norm=final_norm),
        grid=(n // t,),
        in_specs=[
            pl.BlockSpec((t, d), lambda ti: (ti, 0)),
            pl.BlockSpec((t, MIX_W), lambda ti: (ti, 0)),
            resident(wout),
            resident(g),
            resident(wup),
            resident(wdown),
            resident(gfin),
        ],
        out_specs=pl.BlockSpec((t, d), lambda ti: (ti, 0)),
        out_shape=jax.ShapeDtypeStruct((n, d), F32),
        compiler_params=pltpu.CompilerParams(
            dimension_semantics=("arbitrary",),
            vmem_limit_bytes=VMEM_LIMIT_BYTES),
        name="mlp",
    )(x2d, mix2d, wout, g, wup, wdown, gfin)


def _in_proj_offsets():
    widths = [SB_W, SB_W, SB_W, SWA_Q_W, SWA_KV_W, SWA_KV_W]
    offs = [0]
    for w in widths:
        offs.append(offs[-1] + w)
    return offs


def kernel(x, positions, attn_norm, w_in, sb_norm, swa_norm, sinks, w_out, mlp_norm, w_up,
           w_down, final_norm):
    b, s, d = x.shape
    depth = w_in.shape[0]
    scale = 1.0 / math.sqrt(HEAD_DIM)
    tables = _rope_tables(positions)

    r = lax.broadcasted_iota(jnp.int32, (BLOCK, BLOCK), 0)
    c = lax.broadcasted_iota(jnp.int32, (BLOCK, BLOCK), 1)
    tri = jnp.concatenate([jnp.where(r >= c, -1.0, 0.0), jnp.full((BLOCK, BLOCK), -1.0)],
                          axis=1).astype(BF16)
    tri = jnp.concatenate([tri, tri], axis=0)
    rr = lax.broadcasted_iota(jnp.int32, (2 * BLOCK, LANES), 0)
    cc = lax.broadcasted_iota(jnp.int32, (2 * BLOCK, LANES), 1)
    ones_bd = (rr // BLOCK == cc // HEAD_DIM).astype(BF16)

    o = _in_proj_offsets()
    for l in range(depth):
        w = w_in[l]
        wq = (w[:, o[0]:o[1]] * scale).astype(BF16)
        wkt = w[:, o[1]:o[2]].T.astype(BF16)
        wv = w[:, o[2]:o[3]].astype(BF16)
        wsq = (w[:, o[3]:o[4]] * scale).astype(BF16)
        wskt = w[:, o[4]:o[5]].T.astype(BF16)
        wsv = w[:, o[5]:o[6]].astype(BF16)
        sbq, sbkt, sbv_lo, sbv_hi, swq, swkt, swvbd = _in_proj(
            x, attn_norm[l].reshape(1, d), tables, wq, wkt, wv, wsq, wskt, wsv)
        mix = _mixer(sinks[l], sbq, sbkt, sbv_lo, sbv_hi, swq, swkt, swvbd, tri, ones_bd,
                     sb_norm[l].reshape(1, SB_W), swa_norm[l].reshape(1, SWA_Q_W))
        x = _mlp(x.reshape(b * s, d), mix.reshape(b * s, MIX_W), w_out[l].astype(BF16),
                 mlp_norm[l].reshape(1, d), w_up[l].astype(BF16), w_down[l].astype(BF16),
                 final_norm.reshape(1, d), final_norm=(l == depth - 1)).reshape(b, s, d)
    return x
```

```python
import functools
import math

import jax
import jax.numpy as jnp
from jax import lax
from jax.experimental import pallas as pl
from jax.experimental.pallas import tpu as pltpu

D_MODEL = 1024
HEAD_DIM = 64
HALF = HEAD_DIM // 2
SB_HEADS = 8
SWA_HEADS = 8
SWA_KV_HEADS = 2
SWA_GROUP = SWA_HEADS // SWA_KV_HEADS
BLOCK = 128
D_FF = 4 * D_MODEL
ROPE_THETA = 10000.0
EPS = 1e-6
NEG = -1e30
EXP_ZERO_BELOW = -105.0
LOG2_E = 1.4426950408889634

SB_PAIRS = SB_HEADS // 2
SB_UNROLLED_TILES = 3
MIXER_BLOCKS = 4
SWA_BATCH = 2
SB_W = SB_HEADS * HEAD_DIM
SWA_Q_W = SWA_HEADS * HEAD_DIM
SWA_KV_W = SWA_KV_HEADS * HEAD_DIM
SWA_VBD_W = 2 * SWA_KV_HEADS * 2 * HEAD_DIM
MIX_W = SB_W + SWA_Q_W

LANES = 128
VMEM_LIMIT_BYTES = 56 * 1024 * 1024

PROJ_TILE = 1024
PROJ_SPLIT = 2
MLP_TILE = 1024
FF_CHUNK = 1024

BF16 = jnp.bfloat16
F32 = jnp.float32
NT_DIMS = (((1,), (1,)), ((), ()))


def _rms(x, g):
    return x * lax.rsqrt(jnp.mean(x * x, axis=-1, keepdims=True) + EPS) * g


def _rope_table_kernel(pos_col_ref, pos_row_ref, freq_row_ref, freq_col_ref,
                       cos_ref, sin_ref, cos_t_ref, sin_t_ref):
    ang = pos_col_ref[...].astype(F32) * freq_row_ref[...]
    lane = lax.broadcasted_iota(jnp.int32, ang.shape, 1)
    first_half = (lane % HEAD_DIM) < HALF
    cos_ref[...] = jnp.cos(ang)
    s = jnp.sin(ang)
    sin_ref[...] = jnp.where(first_half, -s, s)
    ang_t = freq_col_ref[...] * pos_row_ref[...].astype(F32)
    cos_t_ref[...] = jnp.cos(ang_t)
    sin_t_ref[...] = jnp.sin(ang_t)


def _rope_tables(positions):
    s = positions.shape[0]
    inv_freq = 1.0 / (ROPE_THETA ** (jnp.arange(HALF, dtype=F32) * (2.0 / HEAD_DIM)))
    freq_row = jnp.tile(inv_freq, LANES // HALF).reshape(1, LANES)
    return pl.pallas_call(
        _rope_table_kernel,
        out_shape=(jax.ShapeDtypeStruct((s, LANES), F32),
                   jax.ShapeDtypeStruct((s, LANES), F32),
                   jax.ShapeDtypeStruct((HALF, s), F32),
                   jax.ShapeDtypeStruct((HALF, s), F32)),
        name="rope_tables",
    )(positions.reshape(s, 1), positions.reshape(1, s), freq_row, inv_freq.reshape(HALF, 1))


def _rope(x, cos, sin_signed):
    lane = lax.broadcasted_iota(jnp.int32, (x.shape[0], LANES), 1)
    first_half = (lane % HEAD_DIM) < HALF
    outs = []
    for c in range(x.shape[1] // LANES):
        xc = x[:, c * LANES:(c + 1) * LANES]
        swapped = jnp.where(first_half,
                            pltpu.roll(xc, LANES - HALF, axis=1),
                            pltpu.roll(xc, HALF, axis=1))
        outs.append(xc * cos + swapped * sin_signed)
    return jnp.concatenate(outs, axis=1) if len(outs) > 1 else outs[0]


def _rope_t(xt, cos_t, sin_t):
    outs = []
    for h in range(xt.shape[0] // HEAD_DIM):
        x1 = xt[h * HEAD_DIM:h * HEAD_DIM + HALF]
        x2 = xt[h * HEAD_DIM + HALF:(h + 1) * HEAD_DIM]
        outs += [x1 * cos_t - x2 * sin_t, x2 * cos_t + x1 * sin_t]
    return jnp.concatenate(outs, axis=0)


def _in_proj_kernel(x_ref, g_ref, cos_ref, sin_ref, cos_t_ref, sin_t_ref, wtok_ref, wfeat_ref,
                    sbq_ref, sbkt_ref, sbv_lo_ref, sbv_hi_ref, swq_ref, swkt_ref, swvbd_ref):
    wq_ref = wtok_ref.at[:, 0:SB_W]
    wv_ref = wtok_ref.at[:, SB_W:2 * SB_W]
    wsq_ref = wtok_ref.at[:, 2 * SB_W:2 * SB_W + SWA_Q_W]
    wsv_ref = wtok_ref.at[:, 2 * SB_W + SWA_Q_W:]
    wkt_ref = wfeat_ref.at[0:SB_W]
    wskt_ref = wfeat_ref.at[SB_W:]
    dot = functools.partial(jnp.dot, preferred_element_type=F32)
    dot_nt = functools.partial(lax.dot_general, dimension_numbers=NT_DIMS,
                               preferred_element_type=F32)
    t = x_ref.shape[1] // PROJ_SPLIT
    even_head = (lax.broadcasted_iota(jnp.int32, (t, SB_W), 1) // HEAD_DIM) % 2 == 0
    lo = lax.broadcasted_iota(jnp.int32, (t, LANES), 1) < HEAD_DIM
    for part in range(PROJ_SPLIT):
        rows = slice(part * t, (part + 1) * t)
        h = _rms(x_ref[0, rows], g_ref[...]).astype(BF16)
        sbq_ref[0, rows] = dot(h, wq_ref[...]).astype(BF16)
        sbkt_ref[0, :, rows] = dot_nt(wkt_ref[...], h).astype(BF16)
        v = dot(h, wv_ref[...])
        sbv_lo_ref[0, rows] = jnp.where(even_head, v, 0.0).astype(BF16)
        sbv_hi_ref[0, rows] = jnp.where(even_head, 0.0, v).astype(BF16)
        swq_ref[0, rows] = _rope(dot(h, wsq_ref[...]), cos_ref[rows], sin_ref[rows]).astype(BF16)
        swkt_ref[0, :, rows] = _rope_t(dot_nt(wskt_ref[...], h), cos_t_ref[:, rows],
                                       sin_t_ref[:, rows]).astype(BF16)
        sv = dot(h, wsv_ref[...])
        sv_swapped = pltpu.roll(sv, HEAD_DIM, axis=1)
        swvbd_ref[0, rows] = jnp.concatenate(
            [jnp.where(lo, sv, 0.0), jnp.where(lo, 0.0, sv_swapped),
             jnp.where(lo, sv_swapped, 0.0), jnp.where(lo, 0.0, sv)], axis=1).astype(BF16)


def _in_proj(x, g, tables, wtok, wfeat):
    b, s, d = x.shape
    t = PROJ_TILE
    cos, sin, cos_t, sin_t = tables
    const = lambda bi, ti: (0, 0)
    tok = lambda w: pl.BlockSpec((1, t, w), lambda bi, ti: (bi, ti, 0))
    feat = lambda w: pl.BlockSpec((1, w, t), lambda bi, ti: (bi, 0, ti))
    out_shape = (
        jax.ShapeDtypeStruct((b, s, SB_W), BF16),
        jax.ShapeDtypeStruct((b, SB_W, s), BF16),
        jax.ShapeDtypeStruct((b, s, SB_W), BF16),
        jax.ShapeDtypeStruct((b, s, SB_W), BF16),
        jax.ShapeDtypeStruct((b, s, SWA_Q_W), BF16),
        jax.ShapeDtypeStruct((b, SWA_KV_W, s), BF16),
        jax.ShapeDtypeStruct((b, s, SWA_VBD_W), BF16),
    )
    return pl.pallas_call(
        _in_proj_kernel,
        grid=(b, s // t),
        in_specs=[
            tok(d),
            pl.BlockSpec((1, d), const),
            pl.BlockSpec((t, LANES), lambda bi, ti: (ti, 0)),
            pl.BlockSpec((t, LANES), lambda bi, ti: (ti, 0)),
            pl.BlockSpec((HALF, t), lambda bi, ti: (0, ti)),
            pl.BlockSpec((HALF, t), lambda bi, ti: (0, ti)),
            pl.BlockSpec(wtok.shape, const),
            pl.BlockSpec(wfeat.shape, const),
        ],
        out_specs=(tok(SB_W), feat(SB_W), tok(SB_W), tok(SB_W),
                   tok(SWA_Q_W), feat(SWA_KV_W), tok(SWA_VBD_W)),
        out_shape=out_shape,
        compiler_params=pltpu.CompilerParams(
            dimension_semantics=("arbitrary", "arbitrary"),
            vmem_limit_bytes=VMEM_LIMIT_BYTES),
        name="in_proj",
    )(x, g, cos, sin, cos_t, sin_t, wtok, wfeat)


def _block_diag_keys(kt0, kt1):
    z = jnp.zeros_like(kt0)
    return jnp.concatenate([jnp.concatenate([kt0, z], axis=1),
                            jnp.concatenate([z, kt1], axis=1)], axis=0)


def _sb_scores(q, kt_ref, tri, k0, strict=None, run=None):
    zs = []
    for p in range(SB_PAIRS):
        kt = kt_ref[0, p * LANES:(p + 1) * LANES, pl.ds(k0, BLOCK)]
        zp = jnp.dot(q[p], _block_diag_keys(kt[:HEAD_DIM], kt[HEAD_DIM:]),
                     preferred_element_type=F32)
        zs += [zp[:, :BLOCK], zp[:, BLOCK:]]
    z = jnp.stack(zs)
    sp = jnp.maximum(z, 0.0) + jnp.log(1.0 + jnp.exp2(jnp.abs(z) * (-LOG2_E)))
    if strict is not None:
        sp = jnp.where(strict, sp, 0.0)
    sp = sp.reshape(SB_HEADS * BLOCK, BLOCK)
    hi = sp.astype(BF16)
    lo = (sp - hi.astype(F32)).astype(BF16)
    cs = jnp.dot(jnp.concatenate([hi, lo], axis=1), tri, preferred_element_type=F32)
    if run is not None:
        run2d = run.reshape(SB_HEADS * BLOCK, BLOCK)
        cs = cs + jnp.concatenate([run2d, run2d], axis=1)
    incl = cs[:, :BLOCK].reshape(SB_HEADS, BLOCK, BLOCK)
    total = cs[:, BLOCK:].reshape(SB_HEADS, BLOCK, BLOCK)
    return z, incl, total


def _sb_values(vlo_ref, vhi_ref, k0, z, incl, mask=None, valid=None):
    a = jnp.exp(z + incl)
    if mask is not None:
        a = jnp.where(mask, a, 0.0)
    a = a.astype(BF16)
    out = []
    for p in range(SB_PAIRS):
        cols = slice(p * LANES, (p + 1) * LANES)
        vbd = jnp.concatenate([vlo_ref[0, pl.ds(k0, BLOCK), cols],
                               vhi_ref[0, pl.ds(k0, BLOCK), cols]], axis=0)
        if valid is not None:
            vbd = jnp.where(valid, vbd, jnp.zeros_like(vbd))
        a_pair = jnp.concatenate([a[2 * p], a[2 * p + 1]], axis=1)
        out.append(jnp.dot(a_pair, vbd, preferred_element_type=F32))
    return out


def _swa(i0, blk0, nblk, sinks_ref, q_ref, kt_ref, vbd_ref, ones_bd):
    blocks = range(nblk)
    pairs_per_kv = SWA_GROUP // 2
    k0 = [pl.multiple_of(jnp.maximum(i0 - 1 + t, 0) * BLOCK, BLOCK) for t in range(nblk + 1)]
    qpos = lax.broadcasted_iota(jnp.int32, (BLOCK, BLOCK), 0)
    kpos = lax.broadcasted_iota(jnp.int32, (BLOCK, BLOCK), 1)
    m_cur = (kpos <= qpos)[None]
    sink_slot = kpos == 0
    m_prev = [((kpos > qpos) & (i0 > 0) if b == 0 else kpos > qpos)[None] for b in blocks]
    keep_prev = [m | sink_slot[None] for m in m_prev]
    first_key = lax.broadcasted_iota(jnp.int32, (2 * BLOCK, LANES), 0) % BLOCK == 0
    s_cur, s_prev = [[] for _ in blocks], [[] for _ in blocks]
    v_cur, v_prev = [], []
    for g in range(SWA_KV_HEADS):
        rows = slice(g * HEAD_DIM, (g + 1) * HEAD_DIM)
        kbd = []
        for t in range(nblk + 1):
            kt = kt_ref[0, rows, pl.ds(k0[t], BLOCK)]
            kbd.append(_block_diag_keys(kt, kt))
        for b in blocks:
            for pp in range(pairs_per_kv):
                p = g * pairs_per_kv + pp
                q_pair = q_ref[0, (blk0 + b) * BLOCK:(blk0 + b + 1) * BLOCK,
                               p * LANES:(p + 1) * LANES]
                sc = jnp.dot(q_pair, kbd[b + 1], preferred_element_type=F32)
                sp = jnp.dot(q_pair, kbd[b], preferred_element_type=F32)
                s_cur[b] += [sc[:, :BLOCK], sc[:, BLOCK:]]
                s_prev[b] += [sp[:, :BLOCK], sp[:, BLOCK:]]
        cols = slice(2 * g * LANES, 2 * (g + 1) * LANES)
        vc, vp = [], []
        for t in range(nblk + 1):
            v = vbd_ref[0, pl.ds(k0[t], BLOCK), cols]
            v = jnp.concatenate([v[:, :LANES], v[:, LANES:]], axis=0)
            vc.append(jnp.concatenate([v, ones_bd], axis=1))
            v = jnp.where(first_key, jnp.zeros_like(v), v)
            vp.append(jnp.concatenate([v, ones_bd], axis=1))
        v_cur.append(vc)
        v_prev.append(vp)
    fill = jnp.stack([jnp.where(sink_slot, sinks_ref[h], NEG) for h in range(SWA_HEADS)])
    sc = [jnp.where(m_cur, jnp.stack(s_cur[b]), NEG) for b in blocks]
    sp = [jnp.where(m_prev[b], jnp.stack(s_prev[b]), fill) for b in blocks]
    m = [jnp.max(jnp.maximum(sc[b], sp[b]), axis=2, keepdims=True) for b in blocks]
    p_cur = [jnp.where(m_cur, jnp.exp(sc[b] - m[b]), 0.0).astype(BF16) for b in blocks]
    p_prev = [jnp.where(keep_prev[b], jnp.exp(sp[b] - m[b]), 0.0).astype(BF16) for b in blocks]
    outs = []
    for b in blocks:
        out = []
        for p in range(SWA_HEADS // 2):
            g = p // pairs_per_kv
            a_cur = jnp.concatenate([p_cur[b][2 * p], p_cur[b][2 * p + 1]], axis=1)
            a_prev = jnp.concatenate([p_prev[b][2 * p], p_prev[b][2 * p + 1]], axis=1)
            o = (jnp.dot(a_cur, v_cur[g][b + 1], preferred_element_type=F32)
                 + jnp.dot(a_prev, v_prev[g][b], preferred_element_type=F32))
            out.append(o[:, :LANES] / o[:, LANES:])
        outs.append(jnp.concatenate(out, axis=1))
    return outs


def _mixer_kernel(sinks_ref, sbq_ref, sbkt_ref, sbv_lo_ref, sbv_hi_ref,
                  swq_ref, swkt_ref, swvbd_ref, tri_ref, ones_ref, gsb_ref, gsw_ref,
                  o_ref, acc_ref, run_ref):
    i0 = pl.program_id(1) * MIXER_BLOCKS
    pending = []
    for blk0 in range(0, MIXER_BLOCKS, SWA_BATCH):
        sw_o = _swa(i0 + blk0, blk0, SWA_BATCH, sinks_ref, swq_ref, swkt_ref, swvbd_ref,
                    ones_ref[...])
        for b in range(SWA_BATCH):
            o_ref[0, (blk0 + b) * BLOCK:(blk0 + b + 1) * BLOCK, SB_W:] = (
                _rms(sw_o[b], gsw_ref[...]).astype(BF16))
        pending += [
            _sb_block(blk, i0 + blk, sbq_ref, sbkt_ref, sbv_lo_ref, sbv_hi_ref, tri_ref,
                      gsb_ref, o_ref)
            for blk in range(blk0, blk0 + SWA_BATCH)]
    for blk, sweep_rest in enumerate(pending):
        sweep_rest(acc_ref.at[blk], run_ref.at[blk])


def _sb_block(blk, i, sbq_ref, sbkt_ref, sbv_lo_ref, sbv_hi_ref, tri_ref, gsb_ref, o_ref):
    qrows = slice(blk * BLOCK, (blk + 1) * BLOCK)
    tri = tri_ref[...]
    q = [sbq_ref[0, qrows, p * LANES:(p + 1) * LANES] for p in range(SB_PAIRS)]
    scores = functools.partial(_sb_scores, q, sbkt_ref, tri)
    values = functools.partial(_sb_values, sbv_lo_ref, sbv_hi_ref)

    diag = (lax.broadcasted_iota(jnp.int32, (BLOCK, BLOCK), 1)
            < lax.broadcasted_iota(jnp.int32, (BLOCK, BLOCK), 0))[None]
    k0s = [pl.multiple_of(jnp.maximum(i - t, 0) * BLOCK, BLOCK)
           for t in range(SB_UNROLLED_TILES)]
    front, run = [], None
    for t in range(SB_UNROLLED_TILES):
        z, incl, run = scores(k0s[t], strict=diag if t == 0 else None, run=run)
        front.append((z, incl))
    more = (i >= SB_UNROLLED_TILES) & (jnp.max(run) > EXP_ZERO_BELOW)
    acc = None
    for t, (z, incl) in enumerate(front):
        out = values(k0s[t], z, incl, mask=diag if t == 0 else None,
                     valid=None if blk >= t else i >= t)
        acc = out if acc is None else [a + o for a, o in zip(acc, out)]
    acc = jnp.concatenate(acc, axis=1)
    o_ref[0, qrows, :SB_W] = _rms(acc, gsb_ref[...]).astype(BF16)

    def sweep_rest(acc_ref, run_ref):
        def cond(carry):
            return carry[1]

        def body(carry):
            j, _ = carry
            k0 = pl.multiple_of(j * BLOCK, BLOCK)
            z, incl, new_run = scores(k0, run=run_ref[...])
            acc_ref[...] += jnp.concatenate(values(k0, z, incl), axis=1)
            run_ref[...] = new_run
            return j - 1, (j > 0) & (jnp.max(new_run) > EXP_ZERO_BELOW)

        @pl.when(more)
        def _():
            acc_ref[...] = acc
            run_ref[...] = run
            lax.while_loop(cond, body, (i - SB_UNROLLED_TILES, more))
            o_ref[0, qrows, :SB_W] = _rms(acc_ref[...], gsb_ref[...]).astype(BF16)

    return sweep_rest


def _mixer(sinks, sbq, sbkt, sbv_lo, sbv_hi, swq, swkt, swvbd, tri, ones_bd, gsb, gsw):
    b, s, _ = sbq.shape
    const = lambda bi, qi: (0, 0)
    resident = lambda a: pl.BlockSpec(a.shape, const, pipeline_mode=pl.Buffered(1))
    step_rows = MIXER_BLOCKS * BLOCK
    qblk = lambda w: pl.BlockSpec((1, step_rows, w), lambda bi, qi: (bi, qi, 0))
    per_batch = lambda r, c: pl.BlockSpec((1, r, c), lambda bi, qi: (bi, 0, 0))
    return pl.pallas_call(
        _mixer_kernel,
        grid=(b, s // step_rows),
        in_specs=[
            pl.BlockSpec(memory_space=pltpu.SMEM),
            qblk(SB_W),
            per_batch(SB_W, s),
            per_batch(s, SB_W),
            per_batch(s, SB_W),
            qblk(SWA_Q_W),
            per_batch(SWA_KV_W, s),
            per_batch(s, SWA_VBD_W),
            resident(tri),
            resident(ones_bd),
            resident(gsb),
            resident(gsw),
        ],
        out_specs=qblk(MIX_W),
        out_shape=jax.ShapeDtypeStruct((b, s, MIX_W), BF16),
        scratch_shapes=[pltpu.VMEM((MIXER_BLOCKS, BLOCK, SB_W), F32),
                        pltpu.VMEM((MIXER_BLOCKS, SB_HEADS, BLOCK, BLOCK), F32)],
        compiler_params=pltpu.CompilerParams(
            dimension_semantics=("arbitrary", "arbitrary"),
            vmem_limit_bytes=VMEM_LIMIT_BYTES),
        name="mixer",
    )(sinks, sbq, sbkt, sbv_lo, sbv_hi, swq, swkt, swvbd, tri, ones_bd, gsb, gsw)


def _mlp_kernel(x_ref, mix_ref, wout_ref, g_ref, wup_ref, wdown_ref, gfin_ref, o_ref, *,
                final_norm):
    x = x_ref[...] + jnp.dot(mix_ref[...], wout_ref[...], preferred_element_type=F32)
    h = _rms(x, g_ref[...]).astype(BF16)
    acc = x
    for c in range(D_FF // FF_CHUNK):
        u = jnp.dot(h, wup_ref[:, c * FF_CHUNK:(c + 1) * FF_CHUNK],
                    preferred_element_type=F32)
        u = jnp.square(jnp.maximum(u, 0.0)).astype(BF16)
        acc = acc + jnp.dot(u, wdown_ref[c * FF_CHUNK:(c + 1) * FF_CHUNK, :],
                            preferred_element_type=F32)
    if final_norm:
        acc = _rms(acc, gfin_ref[...])
    o_ref[...] = acc


def _mlp(x2d, mix2d, wout, g, wup, wdown, gfin, final_norm):
    n, d = x2d.shape
    t = MLP_TILE
    const = lambda ti: (0, 0)
    resident = lambda a: pl.BlockSpec(a.shape, const, pipeline_mode=pl.Buffered(1))
    return pl.pallas_call(
        functools.partial(_mlp_kernel, final_norm=final_norm),
        grid=(n // t,),
        in_specs=[
            pl.BlockSpec((t, d), lambda ti: (ti, 0)),
            pl.BlockSpec((t, MIX_W), lambda ti: (ti, 0)),
            resident(wout),
            resident(g),
            resident(wup),
            resident(wdown),
            resident(gfin),
        ],
        out_specs=pl.BlockSpec((t, d), lambda ti: (ti, 0)),
        out_shape=jax.ShapeDtypeStruct((n, d), F32),
        compiler_params=pltpu.CompilerParams(
            dimension_semantics=("arbitrary",),
            vmem_limit_bytes=VMEM_LIMIT_BYTES),
        name="mlp",
    )(x2d, mix2d, wout, g, wup, wdown, gfin)


def _in_proj_offsets():
    widths = [SB_W, SB_W, SB_W, SWA_Q_W, SWA_KV_W, SWA_KV_W]
    offs = [0]
    for w in widths:
        offs.append(offs[-1] + w)
    return offs


def kernel(x, positions, attn_norm, w_in, sb_norm, swa_norm, sinks, w_out, mlp_norm, w_up,
           w_down, final_norm):
    b, s, d = x.shape
    depth = w_in.shape[0]
    scale = 1.0 / math.sqrt(HEAD_DIM)
    tables = _rope_tables(positions)

    r = lax.broadcasted_iota(jnp.int32, (BLOCK, BLOCK), 0)
    c = lax.broadcasted_iota(jnp.int32, (BLOCK, BLOCK), 1)
    tri = jnp.concatenate([jnp.where(r >= c, -1.0, 0.0), jnp.full((BLOCK, BLOCK), -1.0)],
                          axis=1).astype(BF16)
    tri = jnp.concatenate([tri, tri], axis=0)
    rr = lax.broadcasted_iota(jnp.int32, (2 * BLOCK, LANES), 0)
    cc = lax.broadcasted_iota(jnp.int32, (2 * BLOCK, LANES), 1)
    ones_bd = (rr // BLOCK == cc // HEAD_DIM).astype(BF16)

    o = _in_proj_offsets()
    wtok = jnp.concatenate([w_in[:, :, o[0]:o[1]] * scale, w_in[:, :, o[2]:o[3]],
                            w_in[:, :, o[3]:o[4]] * scale, w_in[:, :, o[5]:o[6]]],
                           axis=2).astype(BF16)
    wfeat = jnp.swapaxes(jnp.concatenate([w_in[:, :, o[1]:o[2]], w_in[:, :, o[4]:o[5]]], axis=2),
                         1, 2).astype(BF16)
    w_out, w_up, w_down = w_out.astype(BF16), w_up.astype(BF16), w_down.astype(BF16)

    for l in range(depth):
        sbq, sbkt, sbv_lo, sbv_hi, swq, swkt, swvbd = _in_proj(
            x, attn_norm[l].reshape(1, d), tables, wtok[l], wfeat[l])
        mix = _mixer(sinks[l], sbq, sbkt, sbv_lo, sbv_hi, swq, swkt, swvbd, tri, ones_bd,
                     sb_norm[l].reshape(1, SB_W), swa_norm[l].reshape(1, SWA_Q_W))
        x = _mlp(x.reshape(b * s, d), mix.reshape(b * s, MIX_W), w_out[l],
                 mlp_norm[l].reshape(1, d), w_up[l], w_down[l],
                 final_norm.reshape(1, d), final_norm=(l == depth - 1)).reshape(b, s, d)
    return x
```
